```python
import jax, jax.numpy as jnp
from jax import lax
import numpy as np

D_MODEL = 1024
BATCH = 16
SEQ = 2048
DEPTH = 1

GRID_W = 64
CTX_LEN = 256
FOURIER_WIDTH = D_MODEL // 4
FOURIER_GROUPS = 4
FOURIER_GROUP_DIM = FOURIER_WIDTH // FOURIER_GROUPS
RET_WIDTH = D_MODEL - FOURIER_WIDTH
RET_HEADS = 6
RET_V_DIM = RET_WIDTH // RET_HEADS
RET_QK_DIM = RET_V_DIM // 2
RET_QK_WIDTH = RET_HEADS * RET_QK_DIM
RET_CHUNK = 128
ROPE_BASE = 10000.0
D_FF = -(-8 * D_MODEL // (3 * 256)) * 256
PROJ_WIDTH = FOURIER_WIDTH + 2 * RET_QK_WIDTH + 2 * RET_WIDTH
LN_EPS = 1e-6
DEEPNORM_ALPHA = (2.0 * DEPTH) ** 0.25
DEEPNORM_BETA = (8.0 * DEPTH) ** -0.25

kernel_name = "hymba_fnet_retnet_deepnorm_dit"

F32 = jnp.float32


def layer_norm(x, g=None, b=None):
    x32 = x.astype(F32)
    mu = jnp.mean(x32, axis=-1, keepdims=True)
    var = jnp.mean(jnp.square(x32 - mu), axis=-1, keepdims=True)
    y = (x32 - mu) * lax.rsqrt(var + LN_EPS)
    if g is not None:
        y = y * g.astype(F32) + b.astype(F32)
    return y.astype(x.dtype)


def modulate(xn, shift, scale):
    return xn * (1.0 + scale) + shift


def rope_1d(x, pos):
    nf = x.shape[-1] // 2
    freqs = ROPE_BASE ** (-jnp.arange(nf, dtype=F32) / nf)
    ang = pos[:, None] * freqs[None, :]
    cos = jnp.cos(ang)[None, :, None, :]
    sin = jnp.sin(ang)[None, :, None, :]
    x32 = x.astype(F32)
    x1, x2 = x32[..., :nf], x32[..., nf:]
    return jnp.concatenate([x1 * cos - x2 * sin, x1 * sin + x2 * cos], axis=-1).astype(x.dtype)


def axial_rope(x, row_pos, col_pos):
    half = x.shape[-1] // 2
    return jnp.concatenate([rope_1d(x[..., :half], row_pos), rope_1d(x[..., half:], col_pos)], axis=-1)


def chunk_retention(q, k, v, log_gamma, state0, strict):
    B, N, H, dk = q.shape
    dv = v.shape[-1]
    nC = N // RET_CHUNK
    idx = jnp.arange(RET_CHUNK, dtype=F32)
    diff = idx[:, None] - idx[None, :]
    mask = (diff > 0) if strict else (diff >= 0)
    safe = jnp.where(mask, diff, 0.0)
    decay_inner = jnp.where(mask[None], jnp.exp(log_gamma[:, None, None] * safe[None]), 0.0)
    xi = jnp.exp(log_gamma[:, None] * (idx[None, :] + 1.0))
    zeta = jnp.exp(log_gamma[:, None] * (RET_CHUNK - 1.0 - idx[None, :]))
    g_chunk = jnp.exp(log_gamma * RET_CHUNK)

    def to_chunks(t, d):
        return t.astype(F32).reshape(B, nC, RET_CHUNK, H, d).transpose(1, 0, 3, 2, 4)

    qc, kc, vc = to_chunks(q, dk), to_chunks(k, dk), to_chunks(v, dv)

    def step(state, inp):
        qi, ki, vi = inp
        scores = jnp.einsum('bhcd,bhld->bhcl', qi, ki) * decay_inner[None]
        inner = jnp.einsum('bhcl,bhle->bhce', scores, vi)
        cross = jnp.einsum('bhcd,bhde->bhce', qi, state) * xi[None, :, :, None]
        new_state = g_chunk[None, :, None, None] * state + jnp.einsum(
            'bhld,bhle->bhde', ki * zeta[None, :, :, None], vi)
        return new_state, inner + cross

    final, out = lax.scan(step, state0, (qc, kc, vc))
    out = out.transpose(1, 0, 3, 2, 4).reshape(B, N, H, dv)
    return out, final


def bidirectional_retention(q, k, v, lg_f, lg_b, state_f, state_b):
    rev = lambda t: jnp.flip(t, axis=1)
    y_f, fin_f = chunk_retention(q, k, v, lg_f, state_f, False)
    y_b, fin_b = chunk_retention(rev(q), rev(k), rev(v), lg_b, state_b, True)
    return y_f + rev(y_b), fin_f, fin_b


def mixer_inputs(u, w_in):
    p = u @ w_in
    f, q, k, v, g = jnp.split(p, [FOURIER_WIDTH, FOURIER_WIDTH + RET_QK_WIDTH,
                                  FOURIER_WIDTH + 2 * RET_QK_WIDTH,
                                  FOURIER_WIDTH + 2 * RET_QK_WIDTH + RET_WIDTH], axis=-1)
    B, N = u.shape[0], u.shape[1]
    q = q.reshape(B, N, RET_HEADS, RET_QK_DIM)
    k = k.reshape(B, N, RET_HEADS, RET_QK_DIM) * (RET_QK_DIM ** -0.5)
    v = v.reshape(B, N, RET_HEADS, RET_V_DIM)
    return f, q, k, v, g


def fourier_mix(f):
    B, N, _ = f.shape
    fg = f.reshape(B, N, FOURIER_GROUPS, FOURIER_GROUP_DIM).astype(F32)
    out = jnp.fft.fft2(fg, axes=(1, 3), norm="ortho").real
    return out.reshape(B, N, FOURIER_WIDTH).astype(f.dtype)


def gated_group_norm(y, g):
    B, N = y.shape[0], y.shape[1]
    y = y * lax.rsqrt(jnp.mean(jnp.square(y), axis=-1, keepdims=True) + LN_EPS)
    return (y.reshape(B, N, RET_WIDTH) * jax.nn.silu(g.astype(F32))).astype(g.dtype)


def swiglu(u, wg, wu, wd):
    return (jax.nn.silu(u @ wg) * (u @ wu)) @ wd


def setup_inputs(seed: int = 0) -> dict:
    key = jax.random.key(seed)
    ks = jax.random.split(key, 16)
    nrm = lambda k, shape, s: jax.random.normal(k, shape, F32) * s
    h = jnp.arange(RET_HEADS, dtype=F32)
    p = 2.0 ** (-5.0 - h)
    decay_logit = jnp.log((1.0 - p) / p)
    return {
        "x": nrm(ks[0], (BATCH, SEQ, D_MODEL), 1.0),
        "c": nrm(ks[1], (BATCH, D_MODEL), 1.0),
        "ctx": nrm(ks[2], (BATCH, CTX_LEN, D_MODEL), 1.0),
        "c_ctx": nrm(ks[3], (D_MODEL,), 1.0),
        "w_mod": nrm(ks[4], (DEPTH, D_MODEL, 6 * D_MODEL), 0.5 * D_MODEL ** -0.5),
        "b_mod": nrm(ks[5], (DEPTH, 6 * D_MODEL), 0.01),
        "w_in": nrm(ks[6], (DEPTH, D_MODEL, PROJ_WIDTH), D_MODEL ** -0.5),
        "w_out": nrm(ks[7], (DEPTH, D_MODEL, D_MODEL), DEEPNORM_BETA * D_MODEL ** -0.5),
        "decay_fwd": decay_logit[None, :] + nrm(ks[8], (DEPTH, RET_HEADS), 0.1),
        "decay_bwd": decay_logit[None, :] + nrm(ks[9], (DEPTH, RET_HEADS), 0.1),
        "ln1_g": 1.0 + nrm(ks[10], (DEPTH, D_MODEL), 0.02),
        "ln1_b": nrm(ks[11], (DEPTH, D_MODEL), 0.02),
        "w_ffn_gate": nrm(ks[12], (DEPTH, D_MODEL, D_FF), D_MODEL ** -0.5),
        "w_ffn_up": nrm(ks[13], (DEPTH, D_MODEL, D_FF), D_MODEL ** -0.5),
        "w_ffn_down": nrm(ks[14], (DEPTH, D_FF, D_MODEL), DEEPNORM_BETA * D_FF ** -0.5),
        "ln2_g": 1.0 + nrm(ks[15], (DEPTH, D_MODEL), 0.02),
        "ln2_b": nrm(jax.random.fold_in(ks[15], 1), (DEPTH, D_MODEL), 0.02),
    }


def reference(x, c, ctx, c_ctx, w_mod, b_mod, w_in, w_out, decay_fwd, decay_bwd,
              ln1_g, ln1_b, w_ffn_gate, w_ffn_up, w_ffn_down, ln2_g, ln2_b):
    B, N, _ = x.shape
    rows = N // GRID_W
    row_pos = jnp.repeat(jnp.arange(rows), GRID_W).astype(F32)
    col_pos = jnp.tile(jnp.arange(GRID_W), rows).astype(F32)
    state0 = jnp.zeros((B, RET_HEADS, RET_QK_DIM, RET_V_DIM), F32)

    for l in range(DEPTH):
        last = l == DEPTH - 1
        mod_x = jax.nn.silu(c) @ w_mod[l] + b_mod[l]
        mod_c = jax.nn.silu(c_ctx) @ w_mod[l] + b_mod[l]
        sh1, sc1, gt1, sh2, sc2, gt2 = jnp.split(mod_x[:, None, :], 6, axis=-1)
        csh1, csc1, cgt1, csh2, csc2, cgt2 = jnp.split(mod_c[None, None, :], 6, axis=-1)
        lg_f = jax.nn.log_sigmoid(decay_fwd[l].astype(F32))
        lg_b = jax.nn.log_sigmoid(decay_bwd[l].astype(F32))

        uc = modulate(layer_norm(ctx), csh1, csc1)
        fc, qc, kc, vc, gc = mixer_inputs(uc, w_in[l])
        yc, st_f, st_b = bidirectional_retention(qc, kc, vc, lg_f, lg_b, state0, state0)

        u = modulate(layer_norm(x), sh1, sc1)
        f, q, k, v, g = mixer_inputs(u, w_in[l])
        q = axial_rope(q, row_pos, col_pos)
        k = axial_rope(k, row_pos, col_pos)
        y, _, _ = bidirectional_retention(q, k, v, lg_f, lg_b, st_f, st_b)
        mix = jnp.concatenate([fourier_mix(f), gated_group_norm(y, g)], axis=-1) @ w_out[l]
        x = layer_norm(DEEPNORM_ALPHA * x + gt1 * mix, ln1_g[l], ln1_b[l])
        u2 = modulate(layer_norm(x), sh2, sc2)
        x = layer_norm(DEEPNORM_ALPHA * x + gt2 * swiglu(u2, w_ffn_gate[l], w_ffn_up[l], w_ffn_down[l]),
                       ln2_g[l], ln2_b[l])

        if not last:
            mix_c = jnp.concatenate([fourier_mix(fc), gated_group_norm(yc, gc)], axis=-1) @ w_out[l]
            ctx = layer_norm(DEEPNORM_ALPHA * ctx + cgt1 * mix_c, ln1_g[l], ln1_b[l])
            uc2 = modulate(layer_norm(ctx), csh2, csc2)
            ctx = layer_norm(DEEPNORM_ALPHA * ctx + cgt2 * swiglu(uc2, w_ffn_gate[l], w_ffn_up[l], w_ffn_down[l]),
                             ln2_g[l], ln2_b[l])

    return x
```

```python
import functools

import numpy as np
import jax
import jax.numpy as jnp
from jax import lax
from jax.experimental import pallas as pl
from jax.experimental.pallas import tpu as pltpu

F32 = jnp.float32
BF16 = jnp.bfloat16

GRID_W = 64
FOURIER_GROUPS = 4
FOURIER_GROUP_DIM = 64
FOURIER_WIDTH = FOURIER_GROUPS * FOURIER_GROUP_DIM
RET_HEADS = 6
RET_QK_DIM = 64
RET_V_DIM = 128
RET_QK_WIDTH = RET_HEADS * RET_QK_DIM
RET_WIDTH = RET_HEADS * RET_V_DIM
RET_CHUNK = 128
HEAD_PAIRS = RET_HEADS // 2
PAIR_QK = 2 * RET_QK_DIM
PAIR_V = 2 * RET_V_DIM
ROPE_BASE = 10000.0
LN_EPS = 1e-6
DEPTH = 1
DEEPNORM_ALPHA = (2.0 * DEPTH) ** 0.25

COL_F = 0
COL_Q = COL_F + FOURIER_WIDTH
COL_K = COL_Q + RET_QK_WIDTH
COL_V = COL_K + RET_QK_WIDTH
COL_G = COL_V + RET_WIDTH
PROJ_WIDTH = COL_G + RET_WIDTH

V7X_LANES = 128
V7X_VMEM_LIMIT_BYTES = 56 * 1024 * 1024

TOKEN_TILE = 512
FFN_CHUNK = 256
DFT_ROW_TILE = 512
DFT_COL_TILE = 512


def _dot(a, b):
    return jnp.dot(a, b, preferred_element_type=F32)


def _dot_tt(a, b, lhs_dim, rhs_dim):
    return lax.dot_general(a, b, (((lhs_dim,), (rhs_dim,)), ((), ())), preferred_element_type=F32)


def _layer_norm(x):
    mu = jnp.mean(x, axis=-1, keepdims=True)
    xc = x - mu
    var = jnp.mean(xc * xc, axis=-1, keepdims=True)
    return xc * lax.rsqrt(var + LN_EPS)


def _silu(x):
    return x * jax.nn.sigmoid(x)


def _log_sigmoid(x):
    return jnp.minimum(x, 0.0) - jnp.log1p(jnp.exp(-jnp.abs(x)))


@functools.lru_cache(maxsize=None)
def _rope_tables(n_tokens):
    nf = RET_QK_DIM // 4
    t = np.arange(n_tokens)
    row = (t // GRID_W).astype(np.float32)
    col = (t % GRID_W).astype(np.float32)
    freqs = (np.float32(ROPE_BASE) ** (-np.arange(nf, dtype=np.float32) / np.float32(nf))).astype(np.float32)
    ang_r = (row[:, None] * freqs[None, :]).astype(np.float32)
    ang_c = (col[:, None] * freqs[None, :]).astype(np.float32)
    cos_h = np.concatenate([np.cos(ang_r), np.cos(ang_r), np.cos(ang_c), np.cos(ang_c)], axis=1)
    sin_h = np.concatenate([-np.sin(ang_r), np.sin(ang_r), -np.sin(ang_c), np.sin(ang_c)], axis=1)
    reps = V7X_LANES // RET_QK_DIM
    return (np.tile(cos_h, (1, reps)).astype(np.float32), np.tile(sin_h, (1, reps)).astype(np.float32))


@functools.lru_cache(maxsize=None)
def _channel_dft_table():
    m = np.arange(FOURIER_GROUP_DIM)
    ang = 2.0 * np.pi * ((m[:, None] * m[None, :]) % FOURIER_GROUP_DIM) / FOURIER_GROUP_DIM
    scale = FOURIER_GROUP_DIM ** -0.5
    eye = np.eye(FOURIER_GROUPS)
    return np.concatenate([np.kron(eye, np.cos(ang) * scale), np.kron(eye, np.sin(ang) * scale)], axis=1)


@functools.lru_cache(maxsize=None)
def _position_dft_tables(n_tokens):
    k = np.arange(n_tokens, dtype=np.int64)
    ang = 2.0 * np.pi * ((k[:, None] * k[None, :]) % n_tokens) / n_tokens
    scale = n_tokens ** -0.5
    return np.cos(ang) * scale, np.sin(ang) * scale


def _modulation_kernel(c_ref, w_ref, b_ref, o_ref):
    s = _silu(c_ref[...]).astype(BF16)
    o_ref[...] = _dot(s, w_ref[...].astype(BF16)) + b_ref[...]


def _modulation(c_rows, w_mod, b_mod):
    rows, d = c_rows.shape
    width = w_mod.shape[1]
    tn = width // 4
    return pl.pallas_call(
        _modulation_kernel,
        grid=(width // tn,),
        in_specs=[
            pl.BlockSpec((rows, d), lambda j: (0, 0)),
            pl.BlockSpec((d, tn), lambda j: (0, j)),
            pl.BlockSpec((1, tn), lambda j: (0, j)),
        ],
        out_specs=pl.BlockSpec((rows, tn), lambda j: (0, j)),
        out_shape=jax.ShapeDtypeStruct((rows, width), F32),
        compiler_params=pltpu.CompilerParams(vmem_limit_bytes=V7X_VMEM_LIMIT_BYTES),
        name="modulation",
    )(c_rows, w_mod, b_mod.reshape(1, width))


def _block_diag_mask(shape):
    r = lax.broadcasted_iota(jnp.int32, shape, 0)
    c = lax.broadcasted_iota(jnp.int32, shape, 1)
    return (r >= RET_QK_DIM) == (c >= RET_V_DIM)


def _ctx_states_kernel(ctx_ref, sh_ref, sc_ref, w_ref, dfk_ref, dbk_ref, stf_ref, stb_ref):
    length = ctx_ref.shape[1]
    u = (_layer_norm(ctx_ref[0]) * (1.0 + sc_ref[...]) + sh_ref[...]).astype(BF16)
    k = _dot(u, w_ref[:, COL_K:COL_V]) * (RET_QK_DIM ** -0.5)
    v = _dot(u, w_ref[:, COL_V:COL_G]).astype(BF16)
    t = lax.broadcasted_iota(jnp.int32, (length, PAIR_QK), 0).astype(F32)
    bd = _block_diag_mask((PAIR_QK, PAIR_V))
    for p in range(HEAD_PAIRS):
        k2 = k[:, p * PAIR_QK:(p + 1) * PAIR_QK]
        v2 = v[:, p * PAIR_V:(p + 1) * PAIR_V]
        lg_f = _log_sigmoid(dfk_ref[p])
        lg_b = _log_sigmoid(dbk_ref[p])
        w_f = jnp.exp(lg_f * (length - 1.0 - t))
        w_b = jnp.exp(lg_b * t)
        st_f = _dot_tt((k2 * w_f).astype(BF16), v2, 0, 0)
        st_b = _dot_tt((k2 * w_b).astype(BF16), v2, 0, 0)
        stf_ref[0, p] = jnp.where(bd, st_f, 0.0)
        stb_ref[0, p] = jnp.where(bd, st_b, 0.0)


def _ctx_states(ctx, csh, csc, w_in_bf16, dec_f_k, dec_b_k):
    b, length, d = ctx.shape
    st_shape = jax.ShapeDtypeStruct((b, HEAD_PAIRS, PAIR_QK, PAIR_V), F32)
    st_spec = pl.BlockSpec((1, HEAD_PAIRS, PAIR_QK, PAIR_V), lambda i: (i, 0, 0, 0))
    dec_spec = pl.BlockSpec((HEAD_PAIRS, 1, PAIR_QK), lambda i: (0, 0, 0))
    return pl.pallas_call(
        _ctx_states_kernel,
        grid=(b,),
        in_specs=[
            pl.BlockSpec((1, length, d), lambda i: (i, 0, 0)),
            pl.BlockSpec((1, d), lambda i: (0, 0)),
            pl.BlockSpec((1, d), lambda i: (0, 0)),
            pl.BlockSpec((d, PROJ_WIDTH), lambda i: (0, 0)),
            dec_spec,
            dec_spec,
        ],
        out_specs=[st_spec, st_spec],
        out_shape=[st_shape, st_shape],
        compiler_params=pltpu.CompilerParams(vmem_limit_bytes=V7X_VMEM_LIMIT_BYTES),
        name="ctx_states",
    )(ctx, csh, csc, w_in_bf16, dec_f_k, dec_b_k)


def _rope(x, cos, sin_signed, first_of_pair):
    nf = RET_QK_DIM // 4
    outs = []
    for j in range(x.shape[1] // V7X_LANES):
        xj = x[:, j * V7X_LANES:(j + 1) * V7X_LANES]
        partner = jnp.where(first_of_pair, pltpu.roll(xj, V7X_LANES - nf, 1), pltpu.roll(xj, nf, 1))
        outs.append(xj * cos + partner * sin_signed)
    return jnp.concatenate(outs, axis=1)


def _in_proj_kernel(x_ref, sh_ref, sc_ref, w_ref, wch_ref, cos_ref, sin_ref,
                    q_ref, k_ref, v_ref, g_ref, fc_ref, fs_ref):
    u = (_layer_norm(x_ref[0]) * (1.0 + sc_ref[0]) + sh_ref[0]).astype(BF16)

    f = _dot(u, w_ref[:, COL_F:COL_Q]).astype(BF16)
    fcs = _dot(f, wch_ref[...])
    fc_ref[...] = fcs[:, :FOURIER_WIDTH].astype(BF16)
    fs_ref[...] = fcs[:, FOURIER_WIDTH:].astype(BF16)

    cos = cos_ref[...]
    sin_signed = sin_ref[...]
    lane = lax.broadcasted_iota(jnp.int32, cos.shape, 1)
    first_of_pair = jnp.bitwise_and(lane, 2 * (RET_QK_DIM // 4) - 1) < (RET_QK_DIM // 4)
    q = _dot(u, w_ref[:, COL_Q:COL_K])
    q_ref[0] = _rope(q, cos, sin_signed, first_of_pair).astype(BF16)
    k = _dot(u, w_ref[:, COL_K:COL_V]) * (RET_QK_DIM ** -0.5)
    k_ref[0] = _rope(k, cos, sin_signed, first_of_pair).astype(BF16)

    v_ref[0] = _dot(u, w_ref[:, COL_V:COL_G]).astype(BF16)
    g_ref[0] = _silu(_dot(u, w_ref[:, COL_G:PROJ_WIDTH])).astype(BF16)


def _in_proj(x, sh1, sc1, w_in_bf16, w_ch, cos_t, sin_t):
    b, n, d = x.shape
    tm = min(TOKEN_TILE, n)
    tok = lambda width: pl.BlockSpec((1, tm, width), lambda bi, i: (bi, i, 0))
    vec = pl.BlockSpec((1, 1, d), lambda bi, i: (bi, 0, 0))
    four = pl.BlockSpec((tm, FOURIER_WIDTH), lambda bi, i: (i, bi))
    rope = pl.BlockSpec((tm, V7X_LANES), lambda bi, i: (i, 0))
    return pl.pallas_call(
        _in_proj_kernel,
        grid=(b, n // tm),
        in_specs=[
            tok(d), vec, vec,
            pl.BlockSpec((d, PROJ_WIDTH), lambda bi, i: (0, 0)),
            pl.BlockSpec((FOURIER_WIDTH, 2 * FOURIER_WIDTH), lambda bi, i: (0, 0)),
            rope, rope,
        ],
        out_specs=[tok(RET_QK_WIDTH), tok(RET_QK_WIDTH), tok(RET_WIDTH), tok(RET_WIDTH), four, four],
        out_shape=[
            jax.ShapeDtypeStruct((b, n, RET_QK_WIDTH), BF16),
            jax.ShapeDtypeStruct((b, n, RET_QK_WIDTH), BF16),
            jax.ShapeDtypeStruct((b, n, RET_WIDTH), BF16),
            jax.ShapeDtypeStruct((b, n, RET_WIDTH), BF16),
            jax.ShapeDtypeStruct((n, b * FOURIER_WIDTH), BF16),
            jax.ShapeDtypeStruct((n, b * FOURIER_WIDTH), BF16),
        ],
        compiler_params=pltpu.CompilerParams(vmem_limit_bytes=V7X_VMEM_LIMIT_BYTES),
        name="in_proj",
    )(x, sh1, sc1, w_in_bf16, w_ch, cos_t, sin_t)


def _position_dft_kernel(c_ref, s_ref, fc_ref, fs_ref, o_ref):
    c = c_ref[...].astype(BF16)
    s = s_ref[...].astype(BF16)
    o_ref[...] = (_dot(c, fc_ref[...]) - _dot(s, fs_ref[...])).astype(BF16)


def _position_dft(cos_n, sin_n, fc, fs):
    n, cols = fc.shape
    tm = min(DFT_ROW_TILE, n)
    tn = min(DFT_COL_TILE, cols)
    mat = pl.BlockSpec((tm, n), lambda i, j: (i, 0))
    rhs = pl.BlockSpec((n, tn), lambda i, j: (0, j))
    return pl.pallas_call(
        _position_dft_kernel,
        grid=(n // tm, cols // tn),
        in_specs=[mat, mat, rhs, rhs],
        out_specs=pl.BlockSpec((tm, tn), lambda i, j: (i, j)),
        out_shape=jax.ShapeDtypeStruct((n, cols), BF16),
        compiler_params=pltpu.CompilerParams(vmem_limit_bytes=V7X_VMEM_LIMIT_BYTES),
        name="position_dft",
    )(cos_n, sin_n, fc, fs)


def _retention_kernel(q_ref, k_ref, v_ref, g_ref, stf0_ref, stb0_ref, dfk_ref, dbk_ref, dfv_ref, dbv_ref,
                      o_ref, stf_s, stb_s, sf_all, sb_all, tab_s):
    n = q_ref.shape[1]
    n_chunks = n // RET_CHUNK
    c = RET_CHUNK

    lgf_k = _log_sigmoid(dfk_ref[0])
    lgb_k = _log_sigmoid(dbk_ref[0])
    lgf_v = _log_sigmoid(dfv_ref[0])
    lgb_v = _log_sigmoid(dbv_ref[0])

    i_k = lax.broadcasted_iota(jnp.int32, (c, PAIR_QK), 0).astype(F32)
    i_v = lax.broadcasted_iota(jnp.int32, (c, PAIR_V), 0)
    j_v = jnp.bitwise_and(lax.broadcasted_iota(jnp.int32, (c, PAIR_V), 1), RET_V_DIM - 1)
    diff = (i_v - j_v).astype(F32)
    bd = _block_diag_mask((PAIR_QK, PAIR_V))

    tab_s[0] = jnp.where(diff >= 0.0, jnp.exp(lgf_v * jnp.maximum(diff, 0.0)),
                         jnp.exp(lgb_v * jnp.maximum(-diff, 0.0)))
    i_vf = i_v.astype(F32)
    tab_s[1] = jnp.exp(lgf_v * (i_vf + 1.0))
    tab_s[2] = jnp.exp(lgb_v * (c - i_vf))
    tab_s[3] = jnp.where(bd, jnp.exp(lgf_v * float(c)), 0.0)
    tab_s[4] = jnp.where(bd, jnp.exp(lgb_v * float(c)), 0.0)
    zeta_f = jnp.exp(lgf_k * (c - 1.0 - i_k))
    zeta_b = jnp.exp(lgb_k * i_k)

    stf_s[...] = stf0_ref[0, 0]
    stb_s[...] = stb0_ref[0, 0]

    def chunk(ref, i):
        return ref[0, pl.ds(pl.multiple_of(i * c, c), c), :]

    def state_step(i, carry):
        j = n_chunks - 1 - i
        sf_all[i] = stf_s[...].astype(BF16)
        kz = (chunk(k_ref, i).astype(F32) * zeta_f).astype(BF16)
        upd = _dot_tt(kz, chunk(v_ref, i), 0, 0)
        stf_s[...] = tab_s[3] * stf_s[...] + jnp.where(bd, upd, 0.0)
        sb_all[j] = stb_s[...].astype(BF16)
        kz = (chunk(k_ref, j).astype(F32) * zeta_b).astype(BF16)
        upd = _dot_tt(kz, chunk(v_ref, j), 0, 0)
        stb_s[...] = tab_s[4] * stb_s[...] + jnp.where(bd, upd, 0.0)
        return carry

    lax.fori_loop(0, n_chunks, state_step, 0)

    lane_k = lax.broadcasted_iota(jnp.int32, (c, PAIR_QK), 1)
    lane_v = lax.broadcasted_iota(jnp.int32, (c, PAIR_V), 1)
    zero_k = jnp.zeros((c, PAIR_QK), BF16)
    zero_v = jnp.zeros((c, PAIR_V), BF16)

    def out_step(i, carry):
        q2 = chunk(q_ref, i)
        k2 = chunk(k_ref, i)
        v2 = chunk(v_ref, i)
        k_bd = jnp.concatenate([jnp.where(lane_k < RET_QK_DIM, k2, zero_k),
                                jnp.where(lane_k >= RET_QK_DIM, k2, zero_k)], axis=0)
        v_bd = jnp.concatenate([jnp.where(lane_v < RET_V_DIM, v2, zero_v),
                                jnp.where(lane_v >= RET_V_DIM, v2, zero_v)], axis=0)
        scores = _dot_tt(q2, k_bd, 1, 1)
        p = (scores * tab_s[0]).astype(BF16)
        y = _dot(p, v_bd)
        y = y + _dot(q2, sf_all[i]) * tab_s[1] + _dot(q2, sb_all[i]) * tab_s[2]
        g = chunk(g_ref, i).astype(F32)
        outs = []
        for h in range(2):
            yh = y[:, h * RET_V_DIM:(h + 1) * RET_V_DIM]
            ms = jnp.mean(yh * yh, axis=-1, keepdims=True)
            outs.append(yh * lax.rsqrt(ms + LN_EPS) * g[:, h * RET_V_DIM:(h + 1) * RET_V_DIM])
        o_ref[0, pl.ds(pl.multiple_of(i * c, c), c), :] = jnp.concatenate(outs, axis=1).astype(BF16)
        return carry

    lax.fori_loop(0, n_chunks, out_step, 0)


def _retention(q, k, v, g, st_f, st_b, dec_f_k, dec_b_k, dec_f_v, dec_b_v):
    b, n, _ = q.shape
    n_chunks = n // RET_CHUNK
    qk = pl.BlockSpec((1, n, PAIR_QK), lambda bi, p: (bi, 0, p))
    vg = pl.BlockSpec((1, n, PAIR_V), lambda bi, p: (bi, 0, p))
    st = pl.BlockSpec((1, 1, PAIR_QK, PAIR_V), lambda bi, p: (bi, p, 0, 0))
    dk = pl.BlockSpec((1, 1, PAIR_QK), lambda bi, p: (p, 0, 0))
    dv = pl.BlockSpec((1, 1, PAIR_V), lambda bi, p: (p, 0, 0))
    return pl.pallas_call(
        _retention_kernel,
        grid=(b, HEAD_PAIRS),
        in_specs=[qk, qk, vg, vg, st, st, dk, dk, dv, dv],
        out_specs=vg,
        out_shape=jax.ShapeDtypeStruct((b, n, RET_WIDTH), BF16),
        scratch_shapes=[
            pltpu.VMEM((PAIR_QK, PAIR_V), F32),
            pltpu.VMEM((PAIR_QK, PAIR_V), F32),
            pltpu.VMEM((n_chunks, PAIR_QK, PAIR_V), BF16),
            pltpu.VMEM((n_chunks, PAIR_QK, PAIR_V), BF16),
            pltpu.VMEM((5, RET_CHUNK, PAIR_V), F32),
        ],
        compiler_params=pltpu.CompilerParams(vmem_limit_bytes=V7X_VMEM_LIMIT_BYTES),
        name="retention",
    )(q, k, v, g, st_f, st_b, dec_f_k, dec_b_k, dec_f_v, dec_b_v)


def _out_ffn_kernel(x_ref, fm_ref, yn_ref, gt1_ref, sh2_ref, sc2_ref, gt2_ref,
                    l1g_ref, l1b_ref, l2g_ref, l2b_ref, wo_ref, wg_ref, wu_ref, wd_ref, o_ref):
    mix = _dot(fm_ref[...], wo_ref[:FOURIER_WIDTH, :]) + _dot(yn_ref[0], wo_ref[FOURIER_WIDTH:, :])
    x1 = _layer_norm(DEEPNORM_ALPHA * x_ref[0] + gt1_ref[0] * mix) * l1g_ref[...] + l1b_ref[...]
    u2 = (_layer_norm(x1) * (1.0 + sc2_ref[0]) + sh2_ref[0]).astype(BF16)
    d_ff = wg_ref.shape[1]
    ffn = None
    for c0 in range(0, d_ff, FFN_CHUNK):
        c1 = c0 + FFN_CHUNK
        h = (_silu(_dot(u2, wg_ref[:, c0:c1])) * _dot(u2, wu_ref[:, c0:c1])).astype(BF16)
        part = _dot(h, wd_ref[c0:c1, :])
        ffn = part if ffn is None else ffn + part
    o_ref[0] = _layer_norm(DEEPNORM_ALPHA * x1 + gt2_ref[0] * ffn) * l2g_ref[...] + l2b_ref[...]


def _out_ffn(x, fm, yn, gt1, sh2, sc2, gt2, l1g, l1b, l2g, l2b, wo, wg, wu, wd):
    b, n, d = x.shape
    d_ff = wg.shape[1]
    assert d_ff % FFN_CHUNK == 0
    tm = min(TOKEN_TILE, n)
    tok = lambda width: pl.BlockSpec((1, tm, width), lambda bi, i: (bi, i, 0))
    vec = pl.BlockSpec((1, 1, d), lambda bi, i: (bi, 0, 0))
    par = pl.BlockSpec((1, d), lambda bi, i: (0, 0))
    resident = lambda shape: pl.BlockSpec(shape, lambda bi, i: (0, 0), pipeline_mode=pl.Buffered(1))
    return pl.pallas_call(
        _out_ffn_kernel,
        grid=(b, n // tm),
        in_specs=[
            tok(d),
            pl.BlockSpec((tm, FOURIER_WIDTH), lambda bi, i: (i, bi)),
            tok(RET_WIDTH),
            vec, vec, vec, vec, par, par, par, par,
            resident((d, d)), resident((d, d_ff)), resident((d, d_ff)), resident((d_ff, d)),
        ],
        out_specs=tok(d),
        out_shape=jax.ShapeDtypeStruct((b, n, d), x.dtype),
        compiler_params=pltpu.CompilerParams(vmem_limit_bytes=V7X_VMEM_LIMIT_BYTES),
        name="out_ffn",
    )(x, fm, yn, gt1, sh2, sc2, gt2, l1g, l1b, l2g, l2b, wo, wg, wu, wd)


def _pair_lanes(decay, width):
    return jnp.repeat(decay.astype(F32), width).reshape(HEAD_PAIRS, 1, 2 * width)


def kernel(x, c, ctx, c_ctx, w_mod, b_mod, w_in, w_out, decay_fwd, decay_bwd, ln1_g, ln1_b,
           w_ffn_gate, w_ffn_up, w_ffn_down, ln2_g, ln2_b):
    b, n, d = x.shape
    assert w_mod.shape[0] == DEPTH and n % RET_CHUNK == 0 and n % GRID_W == 0
    assert ctx.shape[1] % RET_CHUNK == 0 and w_in.shape[2] == PROJ_WIDTH

    rows = -(-(b + 1) // 8) * 8
    c_rows = jnp.zeros((rows, d), F32).at[:b].set(c).at[b].set(c_ctx)
    mod = _modulation(c_rows, w_mod[0], b_mod[0])
    sh1, sc1, gt1, sh2, sc2, gt2 = [mod[:b, i * d:(i + 1) * d].reshape(b, 1, d) for i in range(6)]
    csh1 = mod[b:b + 1, 0:d]
    csc1 = mod[b:b + 1, d:2 * d]

    dec_f_k = _pair_lanes(decay_fwd[0], RET_QK_DIM)
    dec_b_k = _pair_lanes(decay_bwd[0], RET_QK_DIM)
    dec_f_v = _pair_lanes(decay_fwd[0], RET_V_DIM)
    dec_b_v = _pair_lanes(decay_bwd[0], RET_V_DIM)

    w_in_bf16 = w_in[0].astype(BF16)
    st_f, st_b = _ctx_states(ctx, csh1, csc1, w_in_bf16, dec_f_k, dec_b_k)

    cos_t, sin_t = _rope_tables(n)
    w_ch = jnp.asarray(_channel_dft_table(), dtype=F32).astype(BF16)
    q, k, v, g, fc, fs = _in_proj(x, sh1, sc1, w_in_bf16, w_ch, jnp.asarray(cos_t), jnp.asarray(sin_t))

    cos_n, sin_n = _position_dft_tables(n)
    fm = _position_dft(jnp.asarray(cos_n, dtype=F32), jnp.asarray(sin_n, dtype=F32), fc, fs)

    yn = _retention(q, k, v, g, st_f, st_b, dec_f_k, dec_b_k, dec_f_v, dec_b_v)

    return _out_ffn(x, fm, yn, gt1, sh2, sc2, gt2,
                    ln1_g[0].reshape(1, d), ln1_b[0].reshape(1, d), ln2_g[0].reshape(1, d), ln2_b[0].reshape(1, d),
                    w_out[0].astype(BF16), w_ffn_gate[0].astype(BF16), w_ffn_up[0].astype(BF16),
                    w_ffn_down[0].astype(BF16))
```

```python
import functools

import numpy as np
import jax
import jax.numpy as jnp
from jax import lax
from jax.experimental import pallas as pl
from jax.experimental.pallas import tpu as pltpu

F32 = jnp.float32
BF16 = jnp.bfloat16

GRID_W = 64
FOURIER_GROUPS = 4
FOURIER_GROUP_DIM = 64
FOURIER_WIDTH = FOURIER_GROUPS * FOURIER_GROUP_DIM
RET_HEADS = 6
RET_QK_DIM = 64
RET_V_DIM = 128
RET_QK_WIDTH = RET_HEADS * RET_QK_DIM
RET_WIDTH = RET_HEADS * RET_V_DIM
RET_CHUNK = 128
HEAD_PAIRS = RET_HEADS // 2
PAIR_QK = 2 * RET_QK_DIM
PAIR_V = 2 * RET_V_DIM
ROPE_BASE = 10000.0
LN_EPS = 1e-6
DEPTH = 1
DEEPNORM_ALPHA = (2.0 * DEPTH) ** 0.25

COL_F = 0
COL_Q = COL_F + FOURIER_WIDTH
COL_K = COL_Q + RET_QK_WIDTH
COL_V = COL_K + RET_QK_WIDTH
COL_G = COL_V + RET_WIDTH
PROJ_WIDTH = COL_G + RET_WIDTH

V7X_LANES = 128
V7X_VMEM_LIMIT_BYTES = 56 * 1024 * 1024

TOKEN_TILE = 512
FFN_CHUNK = 256
DFT_ROW_TILE = 512
DFT_COL_TILE = 512


def _dot(a, b):
    return jnp.dot(a, b, preferred_element_type=F32)


def _dot_tt(a, b, lhs_dim, rhs_dim):
    return lax.dot_general(a, b, (((lhs_dim,), (rhs_dim,)), ((), ())), preferred_element_type=F32)


def _layer_norm(x):
    mu = jnp.mean(x, axis=-1, keepdims=True)
    xc = x - mu
    var = jnp.mean(xc * xc, axis=-1, keepdims=True)
    return xc * lax.rsqrt(var + LN_EPS)


def _silu(x):
    return x * jax.nn.sigmoid(x)


def _log_sigmoid(x):
    return jnp.minimum(x, 0.0) - jnp.log1p(jnp.exp(-jnp.abs(x)))


@functools.lru_cache(maxsize=None)
def _rope_tables(n_tokens):
    nf = RET_QK_DIM // 4
    t = np.arange(n_tokens)
    row = (t // GRID_W).astype(np.float32)
    col = (t % GRID_W).astype(np.float32)
    freqs = (np.float32(ROPE_BASE) ** (-np.arange(nf, dtype=np.float32) / np.float32(nf))).astype(np.float32)
    ang_r = (row[:, None] * freqs[None, :]).astype(np.float32)
    ang_c = (col[:, None] * freqs[None, :]).astype(np.float32)
    cos_h = np.concatenate([np.cos(ang_r), np.cos(ang_r), np.cos(ang_c), np.cos(ang_c)], axis=1)
    sin_h = np.concatenate([-np.sin(ang_r), np.sin(ang_r), -np.sin(ang_c), np.sin(ang_c)], axis=1)
    reps = V7X_LANES // RET_QK_DIM
    return (np.tile(cos_h, (1, reps)).astype(np.float32), np.tile(sin_h, (1, reps)).astype(np.float32))


@functools.lru_cache(maxsize=None)
def _channel_dft_table():
    m = np.arange(FOURIER_GROUP_DIM)
    ang = 2.0 * np.pi * ((m[:, None] * m[None, :]) % FOURIER_GROUP_DIM) / FOURIER_GROUP_DIM
    scale = FOURIER_GROUP_DIM ** -0.5
    eye = np.eye(FOURIER_GROUPS)
    return np.concatenate([np.kron(eye, np.cos(ang) * scale), np.kron(eye, np.sin(ang) * scale)], axis=1)


@functools.lru_cache(maxsize=None)
def _position_dft_tables(n_tokens):
    k = np.arange(n_tokens, dtype=np.int64)
    ang = 2.0 * np.pi * ((k[:, None] * k[None, :]) % n_tokens) / n_tokens
    scale = n_tokens ** -0.5
    return np.cos(ang) * scale, np.sin(ang) * scale


def _modulation_kernel(c_ref, w_ref, b_ref, o_ref):
    s = _silu(c_ref[...]).astype(BF16)
    o_ref[...] = _dot(s, w_ref[...].astype(BF16)) + b_ref[...]


def _modulation(c_rows, w_mod, b_mod):
    rows, d = c_rows.shape
    width = w_mod.shape[1]
    tn = width // 4
    return pl.pallas_call(
        _modulation_kernel,
        grid=(width // tn,),
        in_specs=[
            pl.BlockSpec((rows, d), lambda j: (0, 0)),
            pl.BlockSpec((d, tn), lambda j: (0, j)),
            pl.BlockSpec((1, tn), lambda j: (0, j)),
        ],
        out_specs=pl.BlockSpec((rows, tn), lambda j: (0, j)),
        out_shape=jax.ShapeDtypeStruct((rows, width), F32),
        compiler_params=pltpu.CompilerParams(vmem_limit_bytes=V7X_VMEM_LIMIT_BYTES),
        name="modulation",
    )(c_rows, w_mod, b_mod.reshape(1, width))


def _block_diag_mask(shape):
    r = lax.broadcasted_iota(jnp.int32, shape, 0)
    c = lax.broadcasted_iota(jnp.int32, shape, 1)
    return (r >= RET_QK_DIM) == (c >= RET_V_DIM)


def _ctx_states_kernel(ctx_ref, sh_ref, sc_ref, w_ref, dfk_ref, dbk_ref, stf_ref, stb_ref):
    length = ctx_ref.shape[1]
    u = (_layer_norm(ctx_ref[0]) * (1.0 + sc_ref[...]) + sh_ref[...]).astype(BF16)
    k = _dot(u, w_ref[:, COL_K:COL_V]) * (RET_QK_DIM ** -0.5)
    v = _dot(u, w_ref[:, COL_V:COL_G]).astype(BF16)
    t = lax.broadcasted_iota(jnp.int32, (length, PAIR_QK), 0).astype(F32)
    bd = _block_diag_mask((PAIR_QK, PAIR_V))
    for p in range(HEAD_PAIRS):
        k2 = k[:, p * PAIR_QK:(p + 1) * PAIR_QK]
        v2 = v[:, p * PAIR_V:(p + 1) * PAIR_V]
        lg_f = _log_sigmoid(dfk_ref[p])
        lg_b = _log_sigmoid(dbk_ref[p])
        w_f = jnp.exp(lg_f * (length - 1.0 - t))
        w_b = jnp.exp(lg_b * t)
        st_f = _dot_tt((k2 * w_f).astype(BF16), v2, 0, 0)
        st_b = _dot_tt((k2 * w_b).astype(BF16), v2, 0, 0)
        stf_ref[0, p] = jnp.where(bd, st_f, 0.0)
        stb_ref[0, p] = jnp.where(bd, st_b, 0.0)


def _ctx_states(ctx, csh, csc, w_in_bf16, dec_f_k, dec_b_k):
    b, length, d = ctx.shape
    st_shape = jax.ShapeDtypeStruct((b, HEAD_PAIRS, PAIR_QK, PAIR_V), F32)
    st_spec = pl.BlockSpec((1, HEAD_PAIRS, PAIR_QK, PAIR_V), lambda i: (i, 0, 0, 0))
    dec_spec = pl.BlockSpec((HEAD_PAIRS, 1, PAIR_QK), lambda i: (0, 0, 0))
    return pl.pallas_call(
        _ctx_states_kernel,
        grid=(b,),
        in_specs=[
            pl.BlockSpec((1, length, d), lambda i: (i, 0, 0)),
            pl.BlockSpec((1, d), lambda i: (0, 0)),
            pl.BlockSpec((1, d), lambda i: (0, 0)),
            pl.BlockSpec((d, PROJ_WIDTH), lambda i: (0, 0)),
            dec_spec,
            dec_spec,
        ],
        out_specs=[st_spec, st_spec],
        out_shape=[st_shape, st_shape],
        compiler_params=pltpu.CompilerParams(vmem_limit_bytes=V7X_VMEM_LIMIT_BYTES),
        name="ctx_states",
    )(ctx, csh, csc, w_in_bf16, dec_f_k, dec_b_k)


def _rope(x, cos, sin_signed, first_of_pair):
    nf = RET_QK_DIM // 4
    outs = []
    for j in range(x.shape[1] // V7X_LANES):
        xj = x[:, j * V7X_LANES:(j + 1) * V7X_LANES]
        partner = jnp.where(first_of_pair, pltpu.roll(xj, V7X_LANES - nf, 1), pltpu.roll(xj, nf, 1))
        outs.append(xj * cos + partner * sin_signed)
    return jnp.concatenate(outs, axis=1)


def _in_proj_kernel(x_ref, sh_ref, sc_ref, w_ref, wch_ref, cos_ref, sin_ref,
                    q_ref, k_ref, v_ref, g_ref, fc_ref, fs_ref):
    u = (_layer_norm(x_ref[0]) * (1.0 + sc_ref[0]) + sh_ref[0]).astype(BF16)

    f = _dot(u, w_ref[:, COL_F:COL_Q]).astype(BF16)
    fcs = _dot(f, wch_ref[...])
    fc_ref[...] = fcs[:, :FOURIER_WIDTH].astype(BF16)
    fs_ref[...] = fcs[:, FOURIER_WIDTH:].astype(BF16)

    cos = cos_ref[...]
    sin_signed = sin_ref[...]
    lane = lax.broadcasted_iota(jnp.int32, cos.shape, 1)
    first_of_pair = jnp.bitwise_and(lane, 2 * (RET_QK_DIM // 4) - 1) < (RET_QK_DIM // 4)
    q = _dot(u, w_ref[:, COL_Q:COL_K])
    q_ref[0] = _rope(q, cos, sin_signed, first_of_pair).astype(BF16)
    k = _dot(u, w_ref[:, COL_K:COL_V]) * (RET_QK_DIM ** -0.5)
    k_ref[0] = _rope(k, cos, sin_signed, first_of_pair).astype(BF16)

    v_ref[0] = _dot(u, w_ref[:, COL_V:COL_G]).astype(BF16)
    g_ref[0] = _silu(_dot(u, w_ref[:, COL_G:PROJ_WIDTH])).astype(BF16)


def _in_proj(x, sh1, sc1, w_in_bf16, w_ch, cos_t, sin_t):
    b, n, d = x.shape
    tm = min(TOKEN_TILE, n)
    tok = lambda width: pl.BlockSpec((1, tm, width), lambda bi, i: (bi, i, 0))
    vec = pl.BlockSpec((1, 1, d), lambda bi, i: (bi, 0, 0))
    four = pl.BlockSpec((tm, FOURIER_WIDTH), lambda bi, i: (i, bi))
    rope = pl.BlockSpec((tm, V7X_LANES), lambda bi, i: (i, 0))
    return pl.pallas_call(
        _in_proj_kernel,
        grid=(b, n // tm),
        in_specs=[
            tok(d), vec, vec,
            pl.BlockSpec((d, PROJ_WIDTH), lambda bi, i: (0, 0)),
            pl.BlockSpec((FOURIER_WIDTH, 2 * FOURIER_WIDTH), lambda bi, i: (0, 0)),
            rope, rope,
        ],
        out_specs=[tok(RET_QK_WIDTH), tok(RET_QK_WIDTH), tok(RET_WIDTH), tok(RET_WIDTH), four, four],
        out_shape=[
            jax.ShapeDtypeStruct((b, n, RET_QK_WIDTH), BF16),
            jax.ShapeDtypeStruct((b, n, RET_QK_WIDTH), BF16),
            jax.ShapeDtypeStruct((b, n, RET_WIDTH), BF16),
            jax.ShapeDtypeStruct((b, n, RET_WIDTH), BF16),
            jax.ShapeDtypeStruct((n, b * FOURIER_WIDTH), BF16),
            jax.ShapeDtypeStruct((n, b * FOURIER_WIDTH), BF16),
        ],
        compiler_params=pltpu.CompilerParams(vmem_limit_bytes=V7X_VMEM_LIMIT_BYTES),
        name="in_proj",
    )(x, sh1, sc1, w_in_bf16, w_ch, cos_t, sin_t)


def _position_dft_kernel(c_ref, s_ref, fc_ref, fs_ref, o_ref):
    c = c_ref[...].astype(BF16)
    s = s_ref[...].astype(BF16)
    o_ref[...] = (_dot(c, fc_ref[...]) - _dot(s, fs_ref[...])).astype(BF16)


def _position_dft(cos_n, sin_n, fc, fs):
    n, cols = fc.shape
    tm = min(DFT_ROW_TILE, n)
    tn = min(DFT_COL_TILE, cols)
    mat = pl.BlockSpec((tm, n), lambda i, j: (i, 0))
    rhs = pl.BlockSpec((n, tn), lambda i, j: (0, j))
    return pl.pallas_call(
        _position_dft_kernel,
        grid=(n // tm, cols // tn),
        in_specs=[mat, mat, rhs, rhs],
        out_specs=pl.BlockSpec((tm, tn), lambda i, j: (i, j)),
        out_shape=jax.ShapeDtypeStruct((n, cols), BF16),
        compiler_params=pltpu.CompilerParams(vmem_limit_bytes=V7X_VMEM_LIMIT_BYTES),
        name="position_dft",
    )(cos_n, sin_n, fc, fs)


def _retention_kernel(q_ref, k_ref, v_ref, g_ref, stf0_ref, stb0_ref, dfk_ref, dbk_ref, dfv_ref, dbv_ref,
                      o_ref, u_all, sf_all, sb_all, tab_s):
    n = q_ref.shape[1]
    n_chunks = n // RET_CHUNK
    c = RET_CHUNK

    lgf_k = _log_sigmoid(dfk_ref[0])
    lgb_k = _log_sigmoid(dbk_ref[0])
    lgf_v = _log_sigmoid(dfv_ref[0])
    lgb_v = _log_sigmoid(dbv_ref[0])

    i_k = lax.broadcasted_iota(jnp.int32, (c, PAIR_QK), 0).astype(F32)
    i_v = lax.broadcasted_iota(jnp.int32, (c, PAIR_V), 0)
    j_v = jnp.bitwise_and(lax.broadcasted_iota(jnp.int32, (c, PAIR_V), 1), RET_V_DIM - 1)
    diff = (i_v - j_v).astype(F32)
    bd = _block_diag_mask((PAIR_QK, PAIR_V))

    tab_s[0] = jnp.where(diff >= 0.0, jnp.exp(lgf_v * jnp.maximum(diff, 0.0)),
                         jnp.exp(lgb_v * jnp.maximum(-diff, 0.0)))
    i_vf = i_v.astype(F32)
    tab_s[1] = jnp.exp(lgf_v * (i_vf + 1.0))
    tab_s[2] = jnp.exp(lgb_v * (c - i_vf))
    g_f = jnp.exp(lgf_v * float(c))
    g_b = jnp.exp(lgb_v * float(c))
    zeta_f = jnp.exp(lgf_k * (c - 1.0 - i_k))
    zeta_b = jnp.exp(lgb_k * i_k)

    def chunk(ref, i):
        return ref[0, i * c:(i + 1) * c, :]

    for i in range(n_chunks):
        k2 = chunk(k_ref, i).astype(F32)
        kz = jnp.concatenate([(k2 * zeta_f).astype(BF16), (k2 * zeta_b).astype(BF16)], axis=1)
        u_all[i] = _dot_tt(kz, chunk(v_ref, i), 0, 0)

    s_f = stf0_ref[0, 0]
    s_b = stb0_ref[0, 0]
    for i in range(n_chunks):
        j = n_chunks - 1 - i
        sf_all[i] = jnp.where(bd, s_f, 0.0).astype(BF16)
        sb_all[j] = jnp.where(bd, s_b, 0.0).astype(BF16)
        if i + 1 < n_chunks:
            s_f = g_f * s_f + u_all[i, :PAIR_QK, :]
            s_b = g_b * s_b + u_all[j, PAIR_QK:, :]

    lane_k = lax.broadcasted_iota(jnp.int32, (c, PAIR_QK), 1)
    lane_v = lax.broadcasted_iota(jnp.int32, (c, PAIR_V), 1)
    zero_k = jnp.zeros((c, PAIR_QK), BF16)
    zero_v = jnp.zeros((c, PAIR_V), BF16)

    for i in range(n_chunks):
        q2 = chunk(q_ref, i)
        k2 = chunk(k_ref, i)
        v2 = chunk(v_ref, i)
        k_bd = jnp.concatenate([jnp.where(lane_k < RET_QK_DIM, k2, zero_k),
                                jnp.where(lane_k >= RET_QK_DIM, k2, zero_k)], axis=0)
        v_bd = jnp.concatenate([jnp.where(lane_v < RET_V_DIM, v2, zero_v),
                                jnp.where(lane_v >= RET_V_DIM, v2, zero_v)], axis=0)
        scores = _dot_tt(q2, k_bd, 1, 1)
        p = (scores * tab_s[0]).astype(BF16)
        y = _dot(p, v_bd)
        y = y + _dot(q2, sf_all[i]) * tab_s[1] + _dot(q2, sb_all[i]) * tab_s[2]
        g = chunk(g_ref, i).astype(F32)
        outs = []
        for h in range(2):
            yh = y[:, h * RET_V_DIM:(h + 1) * RET_V_DIM]
            ms = jnp.mean(yh * yh, axis=-1, keepdims=True)
            outs.append(yh * lax.rsqrt(ms + LN_EPS) * g[:, h * RET_V_DIM:(h + 1) * RET_V_DIM])
        o_ref[0, i * c:(i + 1) * c, :] = jnp.concatenate(outs, axis=1).astype(BF16)


def _retention(q, k, v, g, st_f, st_b, dec_f_k, dec_b_k, dec_f_v, dec_b_v):
    b, n, _ = q.shape
    n_chunks = n // RET_CHUNK
    qk = pl.BlockSpec((1, n, PAIR_QK), lambda bi, p: (bi, 0, p))
    vg = pl.BlockSpec((1, n, PAIR_V), lambda bi, p: (bi, 0, p))
    st = pl.BlockSpec((1, 1, PAIR_QK, PAIR_V), lambda bi, p: (bi, p, 0, 0))
    dk = pl.BlockSpec((1, 1, PAIR_QK), lambda bi, p: (p, 0, 0))
    dv = pl.BlockSpec((1, 1, PAIR_V), lambda bi, p: (p, 0, 0))
    return pl.pallas_call(
        _retention_kernel,
        grid=(b, HEAD_PAIRS),
        in_specs=[qk, qk, vg, vg, st, st, dk, dk, dv, dv],
        out_specs=vg,
        out_shape=jax.ShapeDtypeStruct((b, n, RET_WIDTH), BF16),
        scratch_shapes=[
            pltpu.VMEM((n_chunks, 2 * PAIR_QK, PAIR_V), F32),
            pltpu.VMEM((n_chunks, PAIR_QK, PAIR_V), BF16),
            pltpu.VMEM((n_chunks, PAIR_QK, PAIR_V), BF16),
            pltpu.VMEM((3, RET_CHUNK, PAIR_V), F32),
        ],
        compiler_params=pltpu.CompilerParams(vmem_limit_bytes=V7X_VMEM_LIMIT_BYTES),
        name="retention",
    )(q, k, v, g, st_f, st_b, dec_f_k, dec_b_k, dec_f_v, dec_b_v)


def _out_ffn_kernel(x_ref, fm_ref, yn_ref, gt1_ref, sh2_ref, sc2_ref, gt2_ref,
                    l1g_ref, l1b_ref, l2g_ref, l2b_ref, wo_ref, wg_ref, wu_ref, wd_ref, o_ref):
    mix = _dot(fm_ref[...], wo_ref[:FOURIER_WIDTH, :]) + _dot(yn_ref[0], wo_ref[FOURIER_WIDTH:, :])
    x1 = _layer_norm(DEEPNORM_ALPHA * x_ref[0] + gt1_ref[0] * mix) * l1g_ref[...] + l1b_ref[...]
    u2 = (_layer_norm(x1) * (1.0 + sc2_ref[0]) + sh2_ref[0]).astype(BF16)
    d_ff = wg_ref.shape[1]
    ffn = None
    for c0 in range(0, d_ff, FFN_CHUNK):
        c1 = c0 + FFN_CHUNK
        h = (_silu(_dot(u2, wg_ref[:, c0:c1])) * _dot(u2, wu_ref[:, c0:c1])).astype(BF16)
        part = _dot(h, wd_ref[c0:c1, :])
        ffn = part if ffn is None else ffn + part
    o_ref[0] = _layer_norm(DEEPNORM_ALPHA * x1 + gt2_ref[0] * ffn) * l2g_ref[...] + l2b_ref[...]


def _out_ffn(x, fm, yn, gt1, sh2, sc2, gt2, l1g, l1b, l2g, l2b, wo, wg, wu, wd):
    b, n, d = x.shape
    d_ff = wg.shape[1]
    assert d_ff % FFN_CHUNK == 0
    tm = min(TOKEN_TILE, n)
    tok = lambda width: pl.BlockSpec((1, tm, width), lambda bi, i: (bi, i, 0))
    vec = pl.BlockSpec((1, 1, d), lambda bi, i: (bi, 0, 0))
    par = pl.BlockSpec((1, d), lambda bi, i: (0, 0))
    resident = lambda shape: pl.BlockSpec(shape, lambda bi, i: (0, 0), pipeline_mode=pl.Buffered(1))
    return pl.pallas_call(
        _out_ffn_kernel,
        grid=(b, n // tm),
        in_specs=[
            tok(d),
            pl.BlockSpec((tm, FOURIER_WIDTH), lambda bi, i: (i, bi)),
            tok(RET_WIDTH),
            vec, vec, vec, vec, par, par, par, par,
            resident((d, d)), resident((d, d_ff)), resident((d, d_ff)), resident((d_ff, d)),
        ],
        out_specs=tok(d),
        out_shape=jax.ShapeDtypeStruct((b, n, d), x.dtype),
        compiler_params=pltpu.CompilerParams(vmem_limit_bytes=V7X_VMEM_LIMIT_BYTES),
        name="out_ffn",
    )(x, fm, yn, gt1, sh2, sc2, gt2, l1g, l1b, l2g, l2b, wo, wg, wu, wd)


def _pair_lanes(decay, width):
    return jnp.repeat(decay.astype(F32), width).reshape(HEAD_PAIRS, 1, 2 * width)


def kernel(x, c, ctx, c_ctx, w_mod, b_mod, w_in, w_out, decay_fwd, decay_bwd, ln1_g, ln1_b,
           w_ffn_gate, w_ffn_up, w_ffn_down, ln2_g, ln2_b):
    b, n, d = x.shape
    assert w_mod.shape[0] == DEPTH and n % RET_CHUNK == 0 and n % GRID_W == 0
    assert ctx.shape[1] % RET_CHUNK == 0 and w_in.shape[2] == PROJ_WIDTH

    rows = -(-(b + 1) // 8) * 8
    c_rows = jnp.zeros((rows, d), F32).at[:b].set(c).at[b].set(c_ctx)
    mod = _modulation(c_rows, w_mod[0], b_mod[0])
    sh1, sc1, gt1, sh2, sc2, gt2 = [mod[:b, i * d:(i + 1) * d].reshape(b, 1, d) for i in range(6)]
    csh1 = mod[b:b + 1, 0:d]
    csc1 = mod[b:b + 1, d:2 * d]

    dec_f_k = _pair_lanes(decay_fwd[0], RET_QK_DIM)
    dec_b_k = _pair_lanes(decay_bwd[0], RET_QK_DIM)
    dec_f_v = _pair_lanes(decay_fwd[0], RET_V_DIM)
    dec_b_v = _pair_lanes(decay_bwd[0], RET_V_DIM)

    w_in_bf16 = w_in[0].astype(BF16)
    st_f, st_b = _ctx_states(ctx, csh1, csc1, w_in_bf16, dec_f_k, dec_b_k)

    cos_t, sin_t = _rope_tables(n)
    w_ch = jnp.asarray(_channel_dft_table(), dtype=F32).astype(BF16)
    q, k, v, g, fc, fs = _in_proj(x, sh1, sc1, w_in_bf16, w_ch, jnp.asarray(cos_t), jnp.asarray(sin_t))

    cos_n, sin_n = _position_dft_tables(n)
    fm = _position_dft(jnp.asarray(cos_n, dtype=F32), jnp.asarray(sin_n, dtype=F32), fc, fs)

    yn = _retention(q, k, v, g, st_f, st_b, dec_f_k, dec_b_k, dec_f_v, dec_b_v)

    return _out_ffn(x, fm, yn, gt1, sh2, sc2, gt2,
                    ln1_g[0].reshape(1, d), ln1_b[0].reshape(1, d), ln2_g[0].reshape(1, d), ln2_b[0].reshape(1, d),
                    w_out[0].astype(BF16), w_ffn_gate[0].astype(BF16), w_ffn_up[0].astype(BF16),
                    w_ffn_down[0].astype(BF16))
```

```python
import functools

import numpy as np
import jax
import jax.numpy as jnp
from jax import lax
from jax.experimental import pallas as pl
from jax.experimental.pallas import tpu as pltpu

F32 = jnp.float32
BF16 = jnp.bfloat16

GRID_W = 64
FOURIER_GROUPS = 4
FOURIER_GROUP_DIM = 64
FOURIER_WIDTH = FOURIER_GROUPS * FOURIER_GROUP_DIM
RET_HEADS = 6
RET_QK_DIM = 64
RET_V_DIM = 128
RET_QK_WIDTH = RET_HEADS * RET_QK_DIM
RET_WIDTH = RET_HEADS * RET_V_DIM
RET_CHUNK = 128
HEAD_PAIRS = RET_HEADS // 2
PAIR_QK = 2 * RET_QK_DIM
PAIR_V = 2 * RET_V_DIM
ROPE_BASE = 10000.0
LN_EPS = 1e-6
DEPTH = 1
DEEPNORM_ALPHA = (2.0 * DEPTH) ** 0.25

COL_F = 0
COL_Q = COL_F + FOURIER_WIDTH
COL_K = COL_Q + RET_QK_WIDTH
COL_V = COL_K + RET_QK_WIDTH
COL_G = COL_V + RET_WIDTH
PROJ_WIDTH = COL_G + RET_WIDTH

V7X_LANES = 128
V7X_VMEM_LIMIT_BYTES = 56 * 1024 * 1024

TOKEN_TILE = 512
FFN_CHUNK = 256
DFT_ROW_TILE = 512
DFT_COL_TILE = 512


def _dot(a, b):
    return jnp.dot(a, b, preferred_element_type=F32)


def _dot_tt(a, b, lhs_dim, rhs_dim):
    return lax.dot_general(a, b, (((lhs_dim,), (rhs_dim,)), ((), ())), preferred_element_type=F32)


def _layer_norm(x):
    mu = jnp.mean(x, axis=-1, keepdims=True)
    xc = x - mu
    var = jnp.mean(xc * xc, axis=-1, keepdims=True)
    return xc * lax.rsqrt(var + LN_EPS)


def _silu(x):
    return x * jax.nn.sigmoid(x)


def _log_sigmoid(x):
    return jnp.minimum(x, 0.0) - jnp.log1p(jnp.exp(-jnp.abs(x)))


@functools.lru_cache(maxsize=None)
def _rope_tables(n_tokens):
    nf = RET_QK_DIM // 4
    t = np.arange(n_tokens)
    row = (t // GRID_W).astype(np.float32)
    col = (t % GRID_W).astype(np.float32)
    freqs = (np.float32(ROPE_BASE) ** (-np.arange(nf, dtype=np.float32) / np.float32(nf))).astype(np.float32)
    ang_r = (row[:, None] * freqs[None, :]).astype(np.float32)
    ang_c = (col[:, None] * freqs[None, :]).astype(np.float32)
    cos_h = np.concatenate([np.cos(ang_r), np.cos(ang_r), np.cos(ang_c), np.cos(ang_c)], axis=1)
    sin_h = np.concatenate([-np.sin(ang_r), np.sin(ang_r), -np.sin(ang_c), np.sin(ang_c)], axis=1)
    reps = V7X_LANES // RET_QK_DIM
    return (np.tile(cos_h, (1, reps)).astype(np.float32), np.tile(sin_h, (1, reps)).astype(np.float32))


@functools.lru_cache(maxsize=None)
def _channel_dft_table():
    m = np.arange(FOURIER_GROUP_DIM)
    ang = 2.0 * np.pi * ((m[:, None] * m[None, :]) % FOURIER_GROUP_DIM) / FOURIER_GROUP_DIM
    scale = FOURIER_GROUP_DIM ** -0.5
    eye = np.eye(FOURIER_GROUPS)
    return np.concatenate([np.kron(eye, np.cos(ang) * scale), np.kron(eye, np.sin(ang) * scale)], axis=1)


@functools.lru_cache(maxsize=None)
def _position_dft_tables(n_tokens):
    k = np.arange(n_tokens, dtype=np.int64)
    ang = 2.0 * np.pi * ((k[:, None] * k[None, :]) % n_tokens) / n_tokens
    scale = n_tokens ** -0.5
    return np.cos(ang) * scale, np.sin(ang) * scale


def _modulation_kernel(c_ref, w_ref, b_ref, o_ref):
    s = _silu(c_ref[...]).astype(BF16)
    o_ref[...] = _dot(s, w_ref[...].astype(BF16)) + b_ref[...]


def _modulation(c_rows, w_mod, b_mod):
    rows, d = c_rows.shape
    width = w_mod.shape[1]
    tn = width // 4
    return pl.pallas_call(
        _modulation_kernel,
        grid=(width // tn,),
        in_specs=[
            pl.BlockSpec((rows, d), lambda j: (0, 0)),
            pl.BlockSpec((d, tn), lambda j: (0, j)),
            pl.BlockSpec((1, tn), lambda j: (0, j)),
        ],
        out_specs=pl.BlockSpec((rows, tn), lambda j: (0, j)),
        out_shape=jax.ShapeDtypeStruct((rows, width), F32),
        compiler_params=pltpu.CompilerParams(vmem_limit_bytes=V7X_VMEM_LIMIT_BYTES),
        name="modulation",
    )(c_rows, w_mod, b_mod.reshape(1, width))


def _block_diag_mask(shape):
    r = lax.broadcasted_iota(jnp.int32, shape, 0)
    c = lax.broadcasted_iota(jnp.int32, shape, 1)
    return (r >= RET_QK_DIM) == (c >= RET_V_DIM)


def _ctx_states_kernel(ctx_ref, sh_ref, sc_ref, w_ref, dfk_ref, dbk_ref, stf_ref, stb_ref):
    length = ctx_ref.shape[1]
    u = (_layer_norm(ctx_ref[0]) * (1.0 + sc_ref[...]) + sh_ref[...]).astype(BF16)
    k = _dot(u, w_ref[:, COL_K:COL_V]) * (RET_QK_DIM ** -0.5)
    v = _dot(u, w_ref[:, COL_V:COL_G]).astype(BF16)
    t = lax.broadcasted_iota(jnp.int32, (length, PAIR_QK), 0).astype(F32)
    bd = _block_diag_mask((PAIR_QK, PAIR_V))
    for p in range(HEAD_PAIRS):
        k2 = k[:, p * PAIR_QK:(p + 1) * PAIR_QK]
        v2 = v[:, p * PAIR_V:(p + 1) * PAIR_V]
        lg_f = _log_sigmoid(dfk_ref[p])
        lg_b = _log_sigmoid(dbk_ref[p])
        w_f = jnp.exp(lg_f * (length - 1.0 - t))
        w_b = jnp.exp(lg_b * t)
        st_f = _dot_tt((k2 * w_f).astype(BF16), v2, 0, 0)
        st_b = _dot_tt((k2 * w_b).astype(BF16), v2, 0, 0)
        stf_ref[0, p] = jnp.where(bd, st_f, 0.0)
        stb_ref[0, p] = jnp.where(bd, st_b, 0.0)


def _ctx_states(ctx, csh, csc, w_in_bf16, dec_f_k, dec_b_k):
    b, length, d = ctx.shape
    st_shape = jax.ShapeDtypeStruct((b, HEAD_PAIRS, PAIR_QK, PAIR_V), F32)
    st_spec = pl.BlockSpec((1, HEAD_PAIRS, PAIR_QK, PAIR_V), lambda i: (i, 0, 0, 0))
    dec_spec = pl.BlockSpec((HEAD_PAIRS, 1, PAIR_QK), lambda i: (0, 0, 0))
    return pl.pallas_call(
        _ctx_states_kernel,
        grid=(b,),
        in_specs=[
            pl.BlockSpec((1, length, d), lambda i: (i, 0, 0)),
            pl.BlockSpec((1, d), lambda i: (0, 0)),
            pl.BlockSpec((1, d), lambda i: (0, 0)),
            pl.BlockSpec((d, PROJ_WIDTH), lambda i: (0, 0)),
            dec_spec,
            dec_spec,
        ],
        out_specs=[st_spec, st_spec],
        out_shape=[st_shape, st_shape],
        compiler_params=pltpu.CompilerParams(vmem_limit_bytes=V7X_VMEM_LIMIT_BYTES),
        name="ctx_states",
    )(ctx, csh, csc, w_in_bf16, dec_f_k, dec_b_k)


def _rope(x, cos, sin_signed, first_of_pair):
    nf = RET_QK_DIM // 4
    outs = []
    for j in range(x.shape[1] // V7X_LANES):
        xj = x[:, j * V7X_LANES:(j + 1) * V7X_LANES]
        partner = jnp.where(first_of_pair, pltpu.roll(xj, V7X_LANES - nf, 1), pltpu.roll(xj, nf, 1))
        outs.append(xj * cos + partner * sin_signed)
    return jnp.concatenate(outs, axis=1)


def _tile_specs(tile_of_step, tiles_per_batch, tm, d, **kw):
    bi = lambda s: tile_of_step(s) // tiles_per_batch
    ti = lambda s: tile_of_step(s) % tiles_per_batch
    tok = lambda width: pl.BlockSpec((1, tm, width), lambda s: (bi(s), ti(s), 0), **kw)
    vec = pl.BlockSpec((1, 1, d), lambda s: (bi(s), 0, 0), **kw)
    four = pl.BlockSpec((tm, FOURIER_WIDTH), lambda s: (ti(s), bi(s)), **kw)
    rope = pl.BlockSpec((tm, V7X_LANES), lambda s: (ti(s), 0), **kw)
    return tok, vec, four, rope


def _in_proj_kernel(x0_ref, sh0_ref, sc0_ref, xn_ref, shn_ref, scn_ref, w_ref, wch_ref, cos_ref, sin_ref,
                    q_ref, k_ref, v_ref, g_ref, fc_ref, fs_ref, u_a, u_b):
    step = pl.program_id(0)

    def prepare(x_ref, sh_ref, sc_ref, u_dst):
        u_dst[...] = (_layer_norm(x_ref[0]) * (1.0 + sc_ref[0]) + sh_ref[0]).astype(BF16)

    def project(u_src):
        p = _dot(u_src[...], w_ref[...])
        f = p[:, COL_F:COL_Q].astype(BF16)
        fcs = _dot(f, wch_ref[...])
        fc_ref[...] = fcs[:, :FOURIER_WIDTH].astype(BF16)
        fs_ref[...] = fcs[:, FOURIER_WIDTH:].astype(BF16)

        cos = cos_ref[...]
        sin_signed = sin_ref[...]
        lane = lax.broadcasted_iota(jnp.int32, cos.shape, 1)
        first_of_pair = jnp.bitwise_and(lane, 2 * (RET_QK_DIM // 4) - 1) < (RET_QK_DIM // 4)
        q_ref[0] = _rope(p[:, COL_Q:COL_K], cos, sin_signed, first_of_pair).astype(BF16)
        k = p[:, COL_K:COL_V] * (RET_QK_DIM ** -0.5)
        k_ref[0] = _rope(k, cos, sin_signed, first_of_pair).astype(BF16)
        v_ref[0] = p[:, COL_V:COL_G].astype(BF16)
        g_ref[0] = _silu(p[:, COL_G:PROJ_WIDTH]).astype(BF16)

    @pl.when(step == 0)
    def _():
        prepare(x0_ref, sh0_ref, sc0_ref, u_a)

    even = lax.rem(step, 2) == 0

    @pl.when(even)
    def _():
        prepare(xn_ref, shn_ref, scn_ref, u_b)
        project(u_a)

    @pl.when(jnp.logical_not(even))
    def _():
        prepare(xn_ref, shn_ref, scn_ref, u_a)
        project(u_b)


def _in_proj(x, sh1, sc1, w_in_bf16, w_ch, cos_t, sin_t):
    b, n, d = x.shape
    tm = min(TOKEN_TILE, n)
    tiles_per_batch = n // tm
    n_tiles = b * tiles_per_batch
    tok0, vec0, _, _ = _tile_specs(lambda s: 0 * s, tiles_per_batch, tm, d, pipeline_mode=pl.Buffered(1))
    tokn, vecn, _, _ = _tile_specs(lambda s: jnp.minimum(s + 1, n_tiles - 1), tiles_per_batch, tm, d)
    tok, _, four, rope = _tile_specs(lambda s: s, tiles_per_batch, tm, d)
    resident = lambda shape: pl.BlockSpec(shape, lambda s: (0, 0), pipeline_mode=pl.Buffered(1))
    return pl.pallas_call(
        _in_proj_kernel,
        grid=(n_tiles,),
        in_specs=[
            tok0(d), vec0, vec0, tokn(d), vecn, vecn,
            resident((d, PROJ_WIDTH)), resident((FOURIER_WIDTH, 2 * FOURIER_WIDTH)),
            rope, rope,
        ],
        out_specs=[tok(RET_QK_WIDTH), tok(RET_QK_WIDTH), tok(RET_WIDTH), tok(RET_WIDTH), four, four],
        out_shape=[
            jax.ShapeDtypeStruct((b, n, RET_QK_WIDTH), BF16),
            jax.ShapeDtypeStruct((b, n, RET_QK_WIDTH), BF16),
            jax.ShapeDtypeStruct((b, n, RET_WIDTH), BF16),
            jax.ShapeDtypeStruct((b, n, RET_WIDTH), BF16),
            jax.ShapeDtypeStruct((n, b * FOURIER_WIDTH), BF16),
            jax.ShapeDtypeStruct((n, b * FOURIER_WIDTH), BF16),
        ],
        scratch_shapes=[pltpu.VMEM((tm, d), BF16), pltpu.VMEM((tm, d), BF16)],
        compiler_params=pltpu.CompilerParams(
            dimension_semantics=("arbitrary",), vmem_limit_bytes=V7X_VMEM_LIMIT_BYTES),
        name="in_proj",
    )(x, sh1, sc1, x, sh1, sc1, w_in_bf16, w_ch, cos_t, sin_t)


def _position_dft_kernel(c_ref, s_ref, fc_ref, fs_ref, o_ref):
    c = c_ref[...].astype(BF16)
    s = s_ref[...].astype(BF16)
    o_ref[...] = (_dot(c, fc_ref[...]) - _dot(s, fs_ref[...])).astype(BF16)


def _position_dft(cos_n, sin_n, fc, fs):
    n, cols = fc.shape
    tm = min(DFT_ROW_TILE, n)
    tn = min(DFT_COL_TILE, cols)
    mat = pl.BlockSpec((tm, n), lambda i, j: (i, 0))
    rhs = pl.BlockSpec((n, tn), lambda i, j: (0, j))
    return pl.pallas_call(
        _position_dft_kernel,
        grid=(n // tm, cols // tn),
        in_specs=[mat, mat, rhs, rhs],
        out_specs=pl.BlockSpec((tm, tn), lambda i, j: (i, j)),
        out_shape=jax.ShapeDtypeStruct((n, cols), BF16),
        compiler_params=pltpu.CompilerParams(vmem_limit_bytes=V7X_VMEM_LIMIT_BYTES),
        name="position_dft",
    )(cos_n, sin_n, fc, fs)


def _retention_kernel(q_ref, k_ref, v_ref, g_ref, stf0_ref, stb0_ref, dfk_ref, dbk_ref, dfv_ref, dbv_ref,
                      o_ref, u_all, sf_all, sb_all, tab_s):
    n = q_ref.shape[1]
    n_chunks = n // RET_CHUNK
    c = RET_CHUNK

    lgf_k = _log_sigmoid(dfk_ref[0])
    lgb_k = _log_sigmoid(dbk_ref[0])
    lgf_v = _log_sigmoid(dfv_ref[0])
    lgb_v = _log_sigmoid(dbv_ref[0])

    i_k = lax.broadcasted_iota(jnp.int32, (c, PAIR_QK), 0).astype(F32)
    i_v = lax.broadcasted_iota(jnp.int32, (c, PAIR_V), 0)
    j_v = jnp.bitwise_and(lax.broadcasted_iota(jnp.int32, (c, PAIR_V), 1), RET_V_DIM - 1)
    diff = (i_v - j_v).astype(F32)
    bd = _block_diag_mask((PAIR_QK, PAIR_V))

    tab_s[0] = jnp.where(diff >= 0.0, jnp.exp(lgf_v * jnp.maximum(diff, 0.0)),
                         jnp.exp(lgb_v * jnp.maximum(-diff, 0.0)))
    i_vf = i_v.astype(F32)
    tab_s[1] = jnp.exp(lgf_v * (i_vf + 1.0))
    tab_s[2] = jnp.exp(lgb_v * (c - i_vf))
    g_f = jnp.exp(lgf_v * float(c))
    g_b = jnp.exp(lgb_v * float(c))
    zeta_f = jnp.exp(lgf_k * (c - 1.0 - i_k))
    zeta_b = jnp.exp(lgb_k * i_k)

    def chunk(ref, i):
        return ref[0, i * c:(i + 1) * c, :]

    for i in range(n_chunks):
        k2 = chunk(k_ref, i).astype(F32)
        kz = jnp.concatenate([(k2 * zeta_f).astype(BF16), (k2 * zeta_b).astype(BF16)], axis=1)
        u_all[i] = _dot_tt(kz, chunk(v_ref, i), 0, 0)

    s_f = stf0_ref[0, 0]
    s_b = stb0_ref[0, 0]
    for i in range(n_chunks):
        j = n_chunks - 1 - i
        sf_all[i] = jnp.where(bd, s_f, 0.0).astype(BF16)
        sb_all[j] = jnp.where(bd, s_b, 0.0).astype(BF16)
        if i + 1 < n_chunks:
            s_f = g_f * s_f + u_all[i, :PAIR_QK, :]
            s_b = g_b * s_b + u_all[j, PAIR_QK:, :]

    lane_k = lax.broadcasted_iota(jnp.int32, (c, PAIR_QK), 1)
    lane_v = lax.broadcasted_iota(jnp.int32, (c, PAIR_V), 1)
    zero_k = jnp.zeros((c, PAIR_QK), BF16)
    zero_v = jnp.zeros((c, PAIR_V), BF16)

    for i in range(n_chunks):
        q2 = chunk(q_ref, i)
        k2 = chunk(k_ref, i)
        v2 = chunk(v_ref, i)
        k_bd = jnp.concatenate([jnp.where(lane_k < RET_QK_DIM, k2, zero_k),
                                jnp.where(lane_k >= RET_QK_DIM, k2, zero_k)], axis=0)
        v_bd = jnp.concatenate([jnp.where(lane_v < RET_V_DIM, v2, zero_v),
                                jnp.where(lane_v >= RET_V_DIM, v2, zero_v)], axis=0)
        scores = _dot_tt(q2, k_bd, 1, 1)
        p = (scores * tab_s[0]).astype(BF16)
        y = _dot(p, v_bd)
        y = y + _dot(q2, sf_all[i]) * tab_s[1] + _dot(q2, sb_all[i]) * tab_s[2]
        g = chunk(g_ref, i).astype(F32)
        outs = []
        for h in range(2):
            yh = y[:, h * RET_V_DIM:(h + 1) * RET_V_DIM]
            ms = jnp.mean(yh * yh, axis=-1, keepdims=True)
            outs.append(yh * lax.rsqrt(ms + LN_EPS) * g[:, h * RET_V_DIM:(h + 1) * RET_V_DIM])
        o_ref[0, i * c:(i + 1) * c, :] = jnp.concatenate(outs, axis=1).astype(BF16)


def _retention(q, k, v, g, st_f, st_b, dec_f_k, dec_b_k, dec_f_v, dec_b_v):
    b, n, _ = q.shape
    n_chunks = n // RET_CHUNK
    qk = pl.BlockSpec((1, n, PAIR_QK), lambda bi, p: (bi, 0, p))
    vg = pl.BlockSpec((1, n, PAIR_V), lambda bi, p: (bi, 0, p))
    st = pl.BlockSpec((1, 1, PAIR_QK, PAIR_V), lambda bi, p: (bi, p, 0, 0))
    dk = pl.BlockSpec((1, 1, PAIR_QK), lambda bi, p: (p, 0, 0))
    dv = pl.BlockSpec((1, 1, PAIR_V), lambda bi, p: (p, 0, 0))
    return pl.pallas_call(
        _retention_kernel,
        grid=(b, HEAD_PAIRS),
        in_specs=[qk, qk, vg, vg, st, st, dk, dk, dv, dv],
        out_specs=vg,
        out_shape=jax.ShapeDtypeStruct((b, n, RET_WIDTH), BF16),
        scratch_shapes=[
            pltpu.VMEM((n_chunks, 2 * PAIR_QK, PAIR_V), F32),
            pltpu.VMEM((n_chunks, PAIR_QK, PAIR_V), BF16),
            pltpu.VMEM((n_chunks, PAIR_QK, PAIR_V), BF16),
            pltpu.VMEM((3, RET_CHUNK, PAIR_V), F32),
        ],
        compiler_params=pltpu.CompilerParams(vmem_limit_bytes=V7X_VMEM_LIMIT_BYTES),
        name="retention",
    )(q, k, v, g, st_f, st_b, dec_f_k, dec_b_k, dec_f_v, dec_b_v)


def _out_ffn_kernel(x0_ref, fm0_ref, yn0_ref, gt10_ref, sh20_ref, sc20_ref,
                    xn_ref, fmn_ref, ynn_ref, gt1n_ref, sh2n_ref, sc2n_ref,
                    gt2_ref, l1g_ref, l1b_ref, l2g_ref, l2b_ref, wo_ref, wg_ref, wu_ref, wd_ref,
                    o_ref, x1_a, x1_b, u2_a, u2_b):
    step = pl.program_id(0)

    def prepare(x_ref, fm_ref, yn_ref, gt1_ref, sh2_ref, sc2_ref, x1_dst, u2_dst):
        mix = _dot(fm_ref[...], wo_ref[:FOURIER_WIDTH, :]) + _dot(yn_ref[0], wo_ref[FOURIER_WIDTH:, :])
        x1 = _layer_norm(DEEPNORM_ALPHA * x_ref[0] + gt1_ref[0] * mix) * l1g_ref[...] + l1b_ref[...]
        x1_dst[...] = x1
        u2_dst[...] = (_layer_norm(x1) * (1.0 + sc2_ref[0]) + sh2_ref[0]).astype(BF16)

    def ffn(x1_src, u2_src):
        d_ff = wg_ref.shape[1]
        acc = None
        for c0 in range(0, d_ff, FFN_CHUNK):
            c1 = c0 + FFN_CHUNK
            u2 = u2_src[...]
            h = (_silu(_dot(u2, wg_ref[:, c0:c1])) * _dot(u2, wu_ref[:, c0:c1])).astype(BF16)
            part = _dot(h, wd_ref[c0:c1, :])
            acc = part if acc is None else acc + part
        z = DEEPNORM_ALPHA * x1_src[...] + gt2_ref[0] * acc
        o_ref[0] = _layer_norm(z) * l2g_ref[...] + l2b_ref[...]

    @pl.when(step == 0)
    def _():
        prepare(x0_ref, fm0_ref, yn0_ref, gt10_ref, sh20_ref, sc20_ref, x1_a, u2_a)

    even = lax.rem(step, 2) == 0

    @pl.when(even)
    def _():
        prepare(xn_ref, fmn_ref, ynn_ref, gt1n_ref, sh2n_ref, sc2n_ref, x1_b, u2_b)
        ffn(x1_a, u2_a)

    @pl.when(jnp.logical_not(even))
    def _():
        prepare(xn_ref, fmn_ref, ynn_ref, gt1n_ref, sh2n_ref, sc2n_ref, x1_a, u2_a)
        ffn(x1_b, u2_b)


def _out_ffn(x, fm, yn, gt1, sh2, sc2, gt2, l1g, l1b, l2g, l2b, wo, wg, wu, wd):
    b, n, d = x.shape
    d_ff = wg.shape[1]
    assert d_ff % FFN_CHUNK == 0
    tm = min(TOKEN_TILE, n)
    tiles_per_batch = n // tm
    n_tiles = b * tiles_per_batch

    tok0, vec0, four0, _ = _tile_specs(lambda s: 0 * s, tiles_per_batch, tm, d, pipeline_mode=pl.Buffered(1))
    tokn, vecn, fourn, _ = _tile_specs(lambda s: jnp.minimum(s + 1, n_tiles - 1), tiles_per_batch, tm, d)
    tokc, vecc, _, _ = _tile_specs(lambda s: s, tiles_per_batch, tm, d)
    par = pl.BlockSpec((1, d), lambda s: (0, 0))
    resident = lambda shape: pl.BlockSpec(shape, lambda s: (0, 0), pipeline_mode=pl.Buffered(1))
    return pl.pallas_call(
        _out_ffn_kernel,
        grid=(n_tiles,),
        in_specs=[
            tok0(d), four0, tok0(RET_WIDTH), vec0, vec0, vec0,
            tokn(d), fourn, tokn(RET_WIDTH), vecn, vecn, vecn,
            vecc, par, par, par, par,
            resident((d, d)), resident((d, d_ff)), resident((d, d_ff)), resident((d_ff, d)),
        ],
        out_specs=tokc(d),
        out_shape=jax.ShapeDtypeStruct((b, n, d), x.dtype),
        scratch_shapes=[
            pltpu.VMEM((tm, d), F32), pltpu.VMEM((tm, d), F32),
            pltpu.VMEM((tm, d), BF16), pltpu.VMEM((tm, d), BF16),
        ],
        compiler_params=pltpu.CompilerParams(
            dimension_semantics=("arbitrary",), vmem_limit_bytes=V7X_VMEM_LIMIT_BYTES),
        name="out_ffn",
    )(x, fm, yn, gt1, sh2, sc2, x, fm, yn, gt1, sh2, sc2, gt2, l1g, l1b, l2g, l2b, wo, wg, wu, wd)


def _pair_lanes(decay, width):
    return jnp.repeat(decay.astype(F32), width).reshape(HEAD_PAIRS, 1, 2 * width)


def kernel(x, c, ctx, c_ctx, w_mod, b_mod, w_in, w_out, decay_fwd, decay_bwd, ln1_g, ln1_b,
           w_ffn_gate, w_ffn_up, w_ffn_down, ln2_g, ln2_b):
    b, n, d = x.shape
    assert w_mod.shape[0] == DEPTH and n % RET_CHUNK == 0 and n % GRID_W == 0
    assert ctx.shape[1] % RET_CHUNK == 0 and w_in.shape[2] == PROJ_WIDTH

    rows = -(-(b + 1) // 8) * 8
    c_rows = jnp.zeros((rows, d), F32).at[:b].set(c).at[b].set(c_ctx)
    mod = _modulation(c_rows, w_mod[0], b_mod[0])
    sh1, sc1, gt1, sh2, sc2, gt2 = [mod[:b, i * d:(i + 1) * d].reshape(b, 1, d) for i in range(6)]
    csh1 = mod[b:b + 1, 0:d]
    csc1 = mod[b:b + 1, d:2 * d]

    dec_f_k = _pair_lanes(decay_fwd[0], RET_QK_DIM)
    dec_b_k = _pair_lanes(decay_bwd[0], RET_QK_DIM)
    dec_f_v = _pair_lanes(decay_fwd[0], RET_V_DIM)
    dec_b_v = _pair_lanes(decay_bwd[0], RET_V_DIM)

    w_in_bf16 = w_in[0].astype(BF16)
    st_f, st_b = _ctx_states(ctx, csh1, csc1, w_in_bf16, dec_f_k, dec_b_k)

    cos_t, sin_t = _rope_tables(n)
    w_ch = jnp.asarray(_channel_dft_table(), dtype=F32).astype(BF16)
    q, k, v, g, fc, fs = _in_proj(x, sh1, sc1, w_in_bf16, w_ch, jnp.asarray(cos_t), jnp.asarray(sin_t))

    cos_n, sin_n = _position_dft_tables(n)
    fm = _position_dft(jnp.asarray(cos_n, dtype=F32), jnp.asarray(sin_n, dtype=F32), fc, fs)

    yn = _retention(q, k, v, g, st_f, st_b, dec_f_k, dec_b_k, dec_f_v, dec_b_v)

    return _out_ffn(x, fm, yn, gt1, sh2, sc2, gt2,
                    ln1_g[0].reshape(1, d), ln1_b[0].reshape(1, d), ln2_g[0].reshape(1, d), ln2_b[0].reshape(1, d),
                    w_out[0].astype(BF16), w_ffn_gate[0].astype(BF16), w_ffn_up[0].astype(BF16),
                    w_ffn_down[0].astype(BF16))
```

```python
import functools

import numpy as np
import jax
import jax.numpy as jnp
from jax import lax
from jax.experimental import pallas as pl
from jax.experimental.pallas import tpu as pltpu

F32 = jnp.float32
BF16 = jnp.bfloat16

GRID_W = 64
FOURIER_GROUPS = 4
FOURIER_GROUP_DIM = 64
FOURIER_WIDTH = FOURIER_GROUPS * FOURIER_GROUP_DIM
RET_HEADS = 6
RET_QK_DIM = 64
RET_V_DIM = 128
RET_QK_WIDTH = RET_HEADS * RET_QK_DIM
RET_WIDTH = RET_HEADS * RET_V_DIM
RET_CHUNK = 128
HEAD_PAIRS = RET_HEADS // 2
PAIR_QK = 2 * RET_QK_DIM
PAIR_V = 2 * RET_V_DIM
ROPE_BASE = 10000.0
LN_EPS = 1e-6
DEPTH = 1
DEEPNORM_ALPHA = (2.0 * DEPTH) ** 0.25

COL_F = 0
COL_Q = COL_F + FOURIER_WIDTH
COL_K = COL_Q + RET_QK_WIDTH
COL_V = COL_K + RET_QK_WIDTH
COL_G = COL_V + RET_WIDTH
PROJ_WIDTH = COL_G + RET_WIDTH

V7X_LANES = 128
V7X_VMEM_LIMIT_BYTES = 56 * 1024 * 1024

TOKEN_TILE = 512
FFN_CHUNK = 256
DFT_RADIX = 4


def _dot(a, b):
    return jnp.dot(a, b, preferred_element_type=F32)


def _dot_tt(a, b, lhs_dim, rhs_dim):
    return lax.dot_general(a, b, (((lhs_dim,), (rhs_dim,)), ((), ())), preferred_element_type=F32)


def _layer_norm(x):
    mu = jnp.mean(x, axis=-1, keepdims=True)
    xc = x - mu
    var = jnp.mean(xc * xc, axis=-1, keepdims=True)
    return xc * lax.rsqrt(var + LN_EPS)


def _silu(x):
    return x * jax.nn.sigmoid(x)


def _log_sigmoid(x):
    return jnp.minimum(x, 0.0) - jnp.log1p(jnp.exp(-jnp.abs(x)))


@functools.lru_cache(maxsize=None)
def _rope_tables(n_tokens):
    nf = RET_QK_DIM // 4
    t = np.arange(n_tokens)
    row = (t // GRID_W).astype(np.float32)
    col = (t % GRID_W).astype(np.float32)
    freqs = (np.float32(ROPE_BASE) ** (-np.arange(nf, dtype=np.float32) / np.float32(nf))).astype(np.float32)
    ang_r = (row[:, None] * freqs[None, :]).astype(np.float32)
    ang_c = (col[:, None] * freqs[None, :]).astype(np.float32)
    cos_h = np.concatenate([np.cos(ang_r), np.cos(ang_r), np.cos(ang_c), np.cos(ang_c)], axis=1)
    sin_h = np.concatenate([-np.sin(ang_r), np.sin(ang_r), -np.sin(ang_c), np.sin(ang_c)], axis=1)
    reps = V7X_LANES // RET_QK_DIM
    return (np.tile(cos_h, (1, reps)).astype(np.float32), np.tile(sin_h, (1, reps)).astype(np.float32))


@functools.lru_cache(maxsize=None)
def _channel_dft_table():
    m = np.arange(FOURIER_GROUP_DIM)
    ang = 2.0 * np.pi * ((m[:, None] * m[None, :]) % FOURIER_GROUP_DIM) / FOURIER_GROUP_DIM
    scale = FOURIER_GROUP_DIM ** -0.5
    eye = np.eye(FOURIER_GROUPS)
    return np.concatenate([np.kron(eye, np.cos(ang) * scale), np.kron(eye, np.sin(ang) * scale)], axis=1)


@functools.lru_cache(maxsize=None)
def _position_dft_tables(n_tokens):
    sub = n_tokens // DFT_RADIX
    m = np.arange(sub, dtype=np.int64)
    ang = 2.0 * np.pi * ((m[:, None] * m[None, :]) % sub) / sub
    scale = n_tokens ** -0.5
    mat = np.concatenate([np.cos(ang) * scale, -np.sin(ang) * scale], axis=1)
    k1 = np.arange(1, DFT_RADIX, dtype=np.int64)
    phi = 2.0 * np.pi * ((k1[:, None] * m[None, :]) % n_tokens) / n_tokens
    lanes = np.ones((1, 1, V7X_LANES))
    return mat, np.cos(phi)[:, :, None] * lanes, np.sin(phi)[:, :, None] * lanes


def _modulation_kernel(c_ref, w_ref, b_ref, o_ref):
    s = _silu(c_ref[...]).astype(BF16)
    o_ref[...] = _dot(s, w_ref[...].astype(BF16)) + b_ref[...]


def _modulation(c_rows, w_mod, b_mod):
    rows, d = c_rows.shape
    width = w_mod.shape[1]
    tn = width // 4
    return pl.pallas_call(
        _modulation_kernel,
        grid=(width // tn,),
        in_specs=[
            pl.BlockSpec((rows, d), lambda j: (0, 0)),
            pl.BlockSpec((d, tn), lambda j: (0, j)),
            pl.BlockSpec((1, tn), lambda j: (0, j)),
        ],
        out_specs=pl.BlockSpec((rows, tn), lambda j: (0, j)),
        out_shape=jax.ShapeDtypeStruct((rows, width), F32),
        compiler_params=pltpu.CompilerParams(vmem_limit_bytes=V7X_VMEM_LIMIT_BYTES),
        name="modulation",
    )(c_rows, w_mod, b_mod.reshape(1, width))


def _block_diag_mask(shape):
    r = lax.broadcasted_iota(jnp.int32, shape, 0)
    c = lax.broadcasted_iota(jnp.int32, shape, 1)
    return (r >= RET_QK_DIM) == (c >= RET_V_DIM)


def _ctx_states_kernel(ctx_ref, sh_ref, sc_ref, w_ref, dfk_ref, dbk_ref, stf_ref, stb_ref):
    length = ctx_ref.shape[1]
    u = (_layer_norm(ctx_ref[0]) * (1.0 + sc_ref[...]) + sh_ref[...]).astype(BF16)
    k = _dot(u, w_ref[:, COL_K:COL_V]) * (RET_QK_DIM ** -0.5)
    v = _dot(u, w_ref[:, COL_V:COL_G]).astype(BF16)
    t = lax.broadcasted_iota(jnp.int32, (length, PAIR_QK), 0).astype(F32)
    bd = _block_diag_mask((PAIR_QK, PAIR_V))
    for p in range(HEAD_PAIRS):
        k2 = k[:, p * PAIR_QK:(p + 1) * PAIR_QK]
        v2 = v[:, p * PAIR_V:(p + 1) * PAIR_V]
        lg_f = _log_sigmoid(dfk_ref[p])
        lg_b = _log_sigmoid(dbk_ref[p])
        w_f = jnp.exp(lg_f * (length - 1.0 - t))
        w_b = jnp.exp(lg_b * t)
        st_f = _dot_tt((k2 * w_f).astype(BF16), v2, 0, 0)
        st_b = _dot_tt((k2 * w_b).astype(BF16), v2, 0, 0)
        stf_ref[0, p] = jnp.where(bd, st_f, 0.0)
        stb_ref[0, p] = jnp.where(bd, st_b, 0.0)


def _ctx_states(ctx, csh, csc, w_in_bf16, dec_f_k, dec_b_k):
    b, length, d = ctx.shape
    st_shape = jax.ShapeDtypeStruct((b, HEAD_PAIRS, PAIR_QK, PAIR_V), F32)
    st_spec = pl.BlockSpec((1, HEAD_PAIRS, PAIR_QK, PAIR_V), lambda i: (i, 0, 0, 0))
    dec_spec = pl.BlockSpec((HEAD_PAIRS, 1, PAIR_QK), lambda i: (0, 0, 0))
    return pl.pallas_call(
        _ctx_states_kernel,
        grid=(b,),
        in_specs=[
            pl.BlockSpec((1, length, d), lambda i: (i, 0, 0)),
            pl.BlockSpec((1, d), lambda i: (0, 0)),
            pl.BlockSpec((1, d), lambda i: (0, 0)),
            pl.BlockSpec((d, PROJ_WIDTH), lambda i: (0, 0)),
            dec_spec,
            dec_spec,
        ],
        out_specs=[st_spec, st_spec],
        out_shape=[st_shape, st_shape],
        compiler_params=pltpu.CompilerParams(vmem_limit_bytes=V7X_VMEM_LIMIT_BYTES),
        name="ctx_states",
    )(ctx, csh, csc, w_in_bf16, dec_f_k, dec_b_k)


def _rope(x, cos, sin_signed, first_of_pair):
    nf = RET_QK_DIM // 4
    outs = []
    for j in range(x.shape[1] // V7X_LANES):
        xj = x[:, j * V7X_LANES:(j + 1) * V7X_LANES]
        partner = jnp.where(first_of_pair, pltpu.roll(xj, V7X_LANES - nf, 1), pltpu.roll(xj, nf, 1))
        outs.append(xj * cos + partner * sin_signed)
    return jnp.concatenate(outs, axis=1)


def _tile_specs(tile_of_step, tiles_per_batch, tm, d, **kw):
    bi = lambda s: tile_of_step(s) // tiles_per_batch
    ti = lambda s: tile_of_step(s) % tiles_per_batch
    tok = lambda width: pl.BlockSpec((1, tm, width), lambda s: (bi(s), ti(s), 0), **kw)
    vec = pl.BlockSpec((1, 1, d), lambda s: (bi(s), 0, 0), **kw)
    four = pl.BlockSpec((tm, FOURIER_WIDTH), lambda s: (ti(s), bi(s)), **kw)
    rope = pl.BlockSpec((tm, V7X_LANES), lambda s: (ti(s), 0), **kw)
    return tok, vec, four, rope


def _in_proj_kernel(x0_ref, sh0_ref, sc0_ref, xn_ref, shn_ref, scn_ref, w_ref, wch_ref, cos_ref, sin_ref,
                    q_ref, k_ref, v_ref, g_ref, fc_ref, fs_ref, u_a, u_b):
    step = pl.program_id(0)

    def prepare(x_ref, sh_ref, sc_ref, u_dst):
        u_dst[...] = (_layer_norm(x_ref[0]) * (1.0 + sc_ref[0]) + sh_ref[0]).astype(BF16)

    def project(u_src):
        p = _dot(u_src[...], w_ref[...])
        f = p[:, COL_F:COL_Q].astype(BF16)
        fcs = _dot(f, wch_ref[...])
        fc_ref[...] = fcs[:, :FOURIER_WIDTH].astype(BF16)
        fs_ref[...] = fcs[:, FOURIER_WIDTH:].astype(BF16)

        cos = cos_ref[...]
        sin_signed = sin_ref[...]
        lane = lax.broadcasted_iota(jnp.int32, cos.shape, 1)
        first_of_pair = jnp.bitwise_and(lane, 2 * (RET_QK_DIM // 4) - 1) < (RET_QK_DIM // 4)
        q_ref[0] = _rope(p[:, COL_Q:COL_K], cos, sin_signed, first_of_pair).astype(BF16)
        k = p[:, COL_K:COL_V] * (RET_QK_DIM ** -0.5)
        k_ref[0] = _rope(k, cos, sin_signed, first_of_pair).astype(BF16)
        v_ref[0] = p[:, COL_V:COL_G].astype(BF16)
        g_ref[0] = _silu(p[:, COL_G:PROJ_WIDTH]).astype(BF16)

    @pl.when(step == 0)
    def _():
        prepare(x0_ref, sh0_ref, sc0_ref, u_a)

    even = lax.rem(step, 2) == 0

    @pl.when(even)
    def _():
        prepare(xn_ref, shn_ref, scn_ref, u_b)
        project(u_a)

    @pl.when(jnp.logical_not(even))
    def _():
        prepare(xn_ref, shn_ref, scn_ref, u_a)
        project(u_b)


def _in_proj(x, sh1, sc1, w_in_bf16, w_ch, cos_t, sin_t):
    b, n, d = x.shape
    tm = min(TOKEN_TILE, n)
    tiles_per_batch = n // tm
    n_tiles = b * tiles_per_batch
    tok0, vec0, _, _ = _tile_specs(lambda s: 0 * s, tiles_per_batch, tm, d, pipeline_mode=pl.Buffered(1))
    tokn, vecn, _, _ = _tile_specs(lambda s: jnp.minimum(s + 1, n_tiles - 1), tiles_per_batch, tm, d)
    tok, _, four, rope = _tile_specs(lambda s: s, tiles_per_batch, tm, d)
    resident = lambda shape: pl.BlockSpec(shape, lambda s: (0, 0), pipeline_mode=pl.Buffered(1))
    return pl.pallas_call(
        _in_proj_kernel,
        grid=(n_tiles,),
        in_specs=[
            tok0(d), vec0, vec0, tokn(d), vecn, vecn,
            resident((d, PROJ_WIDTH)), resident((FOURIER_WIDTH, 2 * FOURIER_WIDTH)),
            rope, rope,
        ],
        out_specs=[tok(RET_QK_WIDTH), tok(RET_QK_WIDTH), tok(RET_WIDTH), tok(RET_WIDTH), four, four],
        out_shape=[
            jax.ShapeDtypeStruct((b, n, RET_QK_WIDTH), BF16),
            jax.ShapeDtypeStruct((b, n, RET_QK_WIDTH), BF16),
            jax.ShapeDtypeStruct((b, n, RET_WIDTH), BF16),
            jax.ShapeDtypeStruct((b, n, RET_WIDTH), BF16),
            jax.ShapeDtypeStruct((n, b * FOURIER_WIDTH), BF16),
            jax.ShapeDtypeStruct((n, b * FOURIER_WIDTH), BF16),
        ],
        scratch_shapes=[pltpu.VMEM((tm, d), BF16), pltpu.VMEM((tm, d), BF16)],
        compiler_params=pltpu.CompilerParams(
            dimension_semantics=("arbitrary",), vmem_limit_bytes=V7X_VMEM_LIMIT_BYTES),
        name="in_proj",
    )(x, sh1, sc1, x, sh1, sc1, w_in_bf16, w_ch, cos_t, sin_t)


def _position_dft_kernel(fc_ref, fs_ref, mat_ref, twc_ref, tws_ref, o_ref):
    sub = mat_ref.shape[0]
    blk = lambda ref, j: ref[j * sub:(j + 1) * sub, :].astype(F32)
    c0, c1, c2, c3 = (blk(fc_ref, j) for j in range(DFT_RADIX))
    s0, s1, s2, s3 = (blk(fs_ref, j) for j in range(DFT_RADIX))
    tc0, tc1, tc2, tc3 = c0 + c2, c1 + c3, c0 - c2, c1 - c3
    ts0, ts1, ts2, ts3 = s0 + s2, s1 + s3, s0 - s2, s1 - s3
    parts = [
        (tc0 + tc1, ts0 + ts1),
        (tc2 - ts3, ts2 + tc3),
        (tc0 - tc1, ts0 - ts1),
        (tc2 + ts3, ts2 - tc3),
    ]
    mat = mat_ref[...].astype(BF16)
    reps = fc_ref.shape[1] // V7X_LANES
    for k1, (a, b) in enumerate(parts):
        if k1 > 0:
            cphi = jnp.concatenate([twc_ref[k1 - 1]] * reps, axis=1)
            sphi = jnp.concatenate([tws_ref[k1 - 1]] * reps, axis=1)
            a, b = a * cphi - b * sphi, a * sphi + b * cphi
        rhs = jnp.concatenate([a.astype(BF16), b.astype(BF16)], axis=0)
        o_ref[0, k1] = _dot(mat, rhs).astype(BF16)


def _position_dft(mat, tw_cos, tw_sin, fc, fs, batch):
    n, cols = fc.shape
    sub = n // DFT_RADIX
    width = cols // batch
    col = pl.BlockSpec((n, width), lambda bi: (0, bi))
    resident = lambda shape: pl.BlockSpec(shape, lambda bi: (0,) * len(shape), pipeline_mode=pl.Buffered(1))
    return pl.pallas_call(
        _position_dft_kernel,
        grid=(batch,),
        in_specs=[col, col, resident(mat.shape), resident(tw_cos.shape), resident(tw_sin.shape)],
        out_specs=pl.BlockSpec((1, DFT_RADIX, sub, width), lambda bi: (bi, 0, 0, 0)),
        out_shape=jax.ShapeDtypeStruct((batch, DFT_RADIX, sub, width), BF16),
        compiler_params=pltpu.CompilerParams(vmem_limit_bytes=V7X_VMEM_LIMIT_BYTES),
        name="position_dft",
    )(fc, fs, mat, tw_cos, tw_sin)


def _retention_kernel(q_ref, k_ref, v_ref, g_ref, stf0_ref, stb0_ref, dfk_ref, dbk_ref, dfv_ref, dbv_ref,
                      o_ref, u_all, sf_all, sb_all, tab_s):
    n = q_ref.shape[1]
    n_chunks = n // RET_CHUNK
    c = RET_CHUNK

    lgf_k = _log_sigmoid(dfk_ref[0])
    lgb_k = _log_sigmoid(dbk_ref[0])
    lgf_v = _log_sigmoid(dfv_ref[0])
    lgb_v = _log_sigmoid(dbv_ref[0])

    i_k = lax.broadcasted_iota(jnp.int32, (c, PAIR_QK), 0).astype(F32)
    i_v = lax.broadcasted_iota(jnp.int32, (c, PAIR_V), 0)
    j_v = jnp.bitwise_and(lax.broadcasted_iota(jnp.int32, (c, PAIR_V), 1), RET_V_DIM - 1)
    diff = (i_v - j_v).astype(F32)
    bd = _block_diag_mask((PAIR_QK, PAIR_V))

    tab_s[0] = jnp.where(diff >= 0.0, jnp.exp(lgf_v * jnp.maximum(diff, 0.0)),
                         jnp.exp(lgb_v * jnp.maximum(-diff, 0.0)))
    i_vf = i_v.astype(F32)
    tab_s[1] = jnp.exp(lgf_v * (i_vf + 1.0))
    tab_s[2] = jnp.exp(lgb_v * (c - i_vf))
    g_f = jnp.exp(lgf_v * float(c))
    g_b = jnp.exp(lgb_v * float(c))
    zeta_f = jnp.exp(lgf_k * (c - 1.0 - i_k))
    zeta_b = jnp.exp(lgb_k * i_k)

    def chunk(ref, i):
        return ref[0, i * c:(i + 1) * c, :]

    for i in range(n_chunks):
        k2 = chunk(k_ref, i).astype(F32)
        kz = jnp.concatenate([(k2 * zeta_f).astype(BF16), (k2 * zeta_b).astype(BF16)], axis=1)
        u_all[i] = _dot_tt(kz, chunk(v_ref, i), 0, 0)

    s_f = stf0_ref[0, 0]
    s_b = stb0_ref[0, 0]
    for i in range(n_chunks):
        j = n_chunks - 1 - i
        sf_all[i] = jnp.where(bd, s_f, 0.0).astype(BF16)
        sb_all[j] = jnp.where(bd, s_b, 0.0).astype(BF16)
        if i + 1 < n_chunks:
            s_f = g_f * s_f + u_all[i, :PAIR_QK, :]
            s_b = g_b * s_b + u_all[j, PAIR_QK:, :]

    lane_k = lax.broadcasted_iota(jnp.int32, (c, PAIR_QK), 1)
    lane_v = lax.broadcasted_iota(jnp.int32, (c, PAIR_V), 1)
    zero_k = jnp.zeros((c, PAIR_QK), BF16)
    zero_v = jnp.zeros((c, PAIR_V), BF16)

    for i in range(n_chunks):
        q2 = chunk(q_ref, i)
        k2 = chunk(k_ref, i)
        v2 = chunk(v_ref, i)
        k_bd = jnp.concatenate([jnp.where(lane_k < RET_QK_DIM, k2, zero_k),
                                jnp.where(lane_k >= RET_QK_DIM, k2, zero_k)], axis=0)
        v_bd = jnp.concatenate([jnp.where(lane_v < RET_V_DIM, v2, zero_v),
                                jnp.where(lane_v >= RET_V_DIM, v2, zero_v)], axis=0)
        scores = _dot_tt(q2, k_bd, 1, 1)
        p = (scores * tab_s[0]).astype(BF16)
        y = _dot(p, v_bd)
        y = y + _dot(q2, sf_all[i]) * tab_s[1] + _dot(q2, sb_all[i]) * tab_s[2]
        g = chunk(g_ref, i).astype(F32)
        outs = []
        for h in range(2):
            yh = y[:, h * RET_V_DIM:(h + 1) * RET_V_DIM]
            ms = jnp.mean(yh * yh, axis=-1, keepdims=True)
            outs.append(yh * lax.rsqrt(ms + LN_EPS) * g[:, h * RET_V_DIM:(h + 1) * RET_V_DIM])
        o_ref[0, i * c:(i + 1) * c, :] = jnp.concatenate(outs, axis=1).astype(BF16)


def _retention(q, k, v, g, st_f, st_b, dec_f_k, dec_b_k, dec_f_v, dec_b_v):
    b, n, _ = q.shape
    n_chunks = n // RET_CHUNK
    qk = pl.BlockSpec((1, n, PAIR_QK), lambda bi, p: (bi, 0, p))
    vg = pl.BlockSpec((1, n, PAIR_V), lambda bi, p: (bi, 0, p))
    st = pl.BlockSpec((1, 1, PAIR_QK, PAIR_V), lambda bi, p: (bi, p, 0, 0))
    dk = pl.BlockSpec((1, 1, PAIR_QK), lambda bi, p: (p, 0, 0))
    dv = pl.BlockSpec((1, 1, PAIR_V), lambda bi, p: (p, 0, 0))
    return pl.pallas_call(
        _retention_kernel,
        grid=(b, HEAD_PAIRS),
        in_specs=[qk, qk, vg, vg, st, st, dk, dk, dv, dv],
        out_specs=vg,
        out_shape=jax.ShapeDtypeStruct((b, n, RET_WIDTH), BF16),
        scratch_shapes=[
            pltpu.VMEM((n_chunks, 2 * PAIR_QK, PAIR_V), F32),
            pltpu.VMEM((n_chunks, PAIR_QK, PAIR_V), BF16),
            pltpu.VMEM((n_chunks, PAIR_QK, PAIR_V), BF16),
            pltpu.VMEM((3, RET_CHUNK, PAIR_V), F32),
        ],
        compiler_params=pltpu.CompilerParams(vmem_limit_bytes=V7X_VMEM_LIMIT_BYTES),
        name="retention",
    )(q, k, v, g, st_f, st_b, dec_f_k, dec_b_k, dec_f_v, dec_b_v)


def _out_ffn_kernel(x0_ref, fm0_ref, yn0_ref, gt10_ref, sh20_ref, sc20_ref,
                    xn_ref, fmn_ref, ynn_ref, gt1n_ref, sh2n_ref, sc2n_ref,
                    gt2_ref, l1g_ref, l1b_ref, l2g_ref, l2b_ref, wo_ref, wg_ref, wu_ref, wd_ref,
                    o_ref, x1_a, x1_b, u2_a, u2_b):
    step = pl.program_id(0)

    def prepare(x_ref, fm_ref, yn_ref, gt1_ref, sh2_ref, sc2_ref, x1_dst, u2_dst):
        mix = _dot(fm_ref[...], wo_ref[:FOURIER_WIDTH, :]) + _dot(yn_ref[0], wo_ref[FOURIER_WIDTH:, :])
        x1 = _layer_norm(DEEPNORM_ALPHA * x_ref[0] + gt1_ref[0] * mix) * l1g_ref[...] + l1b_ref[...]
        x1_dst[...] = x1
        u2_dst[...] = (_layer_norm(x1) * (1.0 + sc2_ref[0]) + sh2_ref[0]).astype(BF16)

    def ffn(x1_src, u2_src):
        d_ff = wg_ref.shape[1]
        acc = None
        for c0 in range(0, d_ff, FFN_CHUNK):
            c1 = c0 + FFN_CHUNK
            u2 = u2_src[...]
            h = (_silu(_dot(u2, wg_ref[:, c0:c1])) * _dot(u2, wu_ref[:, c0:c1])).astype(BF16)
            part = _dot(h, wd_ref[c0:c1, :])
            acc = part if acc is None else acc + part
        z = DEEPNORM_ALPHA * x1_src[...] + gt2_ref[0] * acc
        o_ref[0] = _layer_norm(z) * l2g_ref[...] + l2b_ref[...]

    @pl.when(step == 0)
    def _():
        prepare(x0_ref, fm0_ref, yn0_ref, gt10_ref, sh20_ref, sc20_ref, x1_a, u2_a)

    even = lax.rem(step, 2) == 0

    @pl.when(even)
    def _():
        prepare(xn_ref, fmn_ref, ynn_ref, gt1n_ref, sh2n_ref, sc2n_ref, x1_b, u2_b)
        ffn(x1_a, u2_a)

    @pl.when(jnp.logical_not(even))
    def _():
        prepare(xn_ref, fmn_ref, ynn_ref, gt1n_ref, sh2n_ref, sc2n_ref, x1_a, u2_a)
        ffn(x1_b, u2_b)


def _out_ffn(x, fm, yn, gt1, sh2, sc2, gt2, l1g, l1b, l2g, l2b, wo, wg, wu, wd):
    b, n, d = x.shape
    d_ff = wg.shape[1]
    assert d_ff % FFN_CHUNK == 0
    tm = n // DFT_RADIX
    n_tiles = b * DFT_RADIX
    x_sets = x.reshape(b, tm, DFT_RADIX * d)
    yn_sets = yn.reshape(b, tm, DFT_RADIX * RET_WIDTH)

    def specs(tile_of_step, **kw):
        bi = lambda s: tile_of_step(s) // DFT_RADIX
        k1 = lambda s: tile_of_step(s) % DFT_RADIX
        tok = lambda width: pl.BlockSpec((1, tm, width), lambda s: (bi(s), 0, k1(s)), **kw)
        vec = pl.BlockSpec((1, 1, d), lambda s: (bi(s), 0, 0), **kw)
        four = pl.BlockSpec((None, None, tm, FOURIER_WIDTH), lambda s: (bi(s), k1(s), 0, 0), **kw)
        return tok, vec, four

    tok0, vec0, four0 = specs(lambda s: 0 * s, pipeline_mode=pl.Buffered(1))
    tokn, vecn, fourn = specs(lambda s: jnp.minimum(s + 1, n_tiles - 1))
    tokc, vecc, _ = specs(lambda s: s)
    par = pl.BlockSpec((1, d), lambda s: (0, 0))
    resident = lambda shape: pl.BlockSpec(shape, lambda s: (0, 0), pipeline_mode=pl.Buffered(1))
    out = pl.pallas_call(
        _out_ffn_kernel,
        grid=(n_tiles,),
        in_specs=[
            tok0(d), four0, tok0(RET_WIDTH), vec0, vec0, vec0,
            tokn(d), fourn, tokn(RET_WIDTH), vecn, vecn, vecn,
            vecc, par, par, par, par,
            resident((d, d)), resident((d, d_ff)), resident((d, d_ff)), resident((d_ff, d)),
        ],
        out_specs=tokc(d),
        out_shape=jax.ShapeDtypeStruct((b, tm, DFT_RADIX * d), x.dtype),
        scratch_shapes=[
            pltpu.VMEM((tm, d), F32), pltpu.VMEM((tm, d), F32),
            pltpu.VMEM((tm, d), BF16), pltpu.VMEM((tm, d), BF16),
        ],
        compiler_params=pltpu.CompilerParams(
            dimension_semantics=("arbitrary",), vmem_limit_bytes=V7X_VMEM_LIMIT_BYTES),
        name="out_ffn",
    )(x_sets, fm, yn_sets, gt1, sh2, sc2, x_sets, fm, yn_sets, gt1, sh2, sc2, gt2,
      l1g, l1b, l2g, l2b, wo, wg, wu, wd)
    return out.reshape(b, n, d)


def _pair_lanes(decay, width):
    return jnp.repeat(decay.astype(F32), width).reshape(HEAD_PAIRS, 1, 2 * width)


def kernel(x, c, ctx, c_ctx, w_mod, b_mod, w_in, w_out, decay_fwd, decay_bwd, ln1_g, ln1_b,
           w_ffn_gate, w_ffn_up, w_ffn_down, ln2_g, ln2_b):
    b, n, d = x.shape
    assert w_mod.shape[0] == DEPTH and n % RET_CHUNK == 0 and n % GRID_W == 0 and n % (8 * DFT_RADIX) == 0
    assert ctx.shape[1] % RET_CHUNK == 0 and w_in.shape[2] == PROJ_WIDTH

    rows = -(-(b + 1) // 8) * 8
    c_rows = jnp.zeros((rows, d), F32).at[:b].set(c).at[b].set(c_ctx)
    mod = _modulation(c_rows, w_mod[0], b_mod[0])
    sh1, sc1, gt1, sh2, sc2, gt2 = [mod[:b, i * d:(i + 1) * d].reshape(b, 1, d) for i in range(6)]
    csh1 = mod[b:b + 1, 0:d]
    csc1 = mod[b:b + 1, d:2 * d]

    dec_f_k = _pair_lanes(decay_fwd[0], RET_QK_DIM)
    dec_b_k = _pair_lanes(decay_bwd[0], RET_QK_DIM)
    dec_f_v = _pair_lanes(decay_fwd[0], RET_V_DIM)
    dec_b_v = _pair_lanes(decay_bwd[0], RET_V_DIM)

    w_in_bf16 = w_in[0].astype(BF16)
    st_f, st_b = _ctx_states(ctx, csh1, csc1, w_in_bf16, dec_f_k, dec_b_k)

    cos_t, sin_t = _rope_tables(n)
    w_ch = jnp.asarray(_channel_dft_table(), dtype=F32).astype(BF16)
    q, k, v, g, fc, fs = _in_proj(x, sh1, sc1, w_in_bf16, w_ch, jnp.asarray(cos_t), jnp.asarray(sin_t))

    dft_mat, tw_cos, tw_sin = (jnp.asarray(t, dtype=F32) for t in _position_dft_tables(n))
    fm = _position_dft(dft_mat, tw_cos, tw_sin, fc, fs, b)

    yn = _retention(q, k, v, g, st_f, st_b, dec_f_k, dec_b_k, dec_f_v, dec_b_v)

    return _out_ffn(x, fm, yn, gt1, sh2, sc2, gt2,
                    ln1_g[0].reshape(1, d), ln1_b[0].reshape(1, d), ln2_g[0].reshape(1, d), ln2_b[0].reshape(1, d),
                    w_out[0].astype(BF16), w_ffn_gate[0].astype(BF16), w_ffn_up[0].astype(BF16),
                    w_ffn_down[0].astype(BF16))
```

```python
import functools

import numpy as np
import jax
import jax.numpy as jnp
from jax import lax
from jax.experimental import pallas as pl
from jax.experimental.pallas import tpu as pltpu

F32 = jnp.float32
BF16 = jnp.bfloat16

GRID_W = 64
FOURIER_GROUPS = 4
FOURIER_GROUP_DIM = 64
FOURIER_WIDTH = FOURIER_GROUPS * FOURIER_GROUP_DIM
RET_HEADS = 6
RET_QK_DIM = 64
RET_V_DIM = 128
RET_QK_WIDTH = RET_HEADS * RET_QK_DIM
RET_WIDTH = RET_HEADS * RET_V_DIM
RET_CHUNK = 128
HEAD_PAIRS = RET_HEADS // 2
PAIR_QK = 2 * RET_QK_DIM
PAIR_V = 2 * RET_V_DIM
ROPE_BASE = 10000.0
LN_EPS = 1e-6
DEPTH = 1
DEEPNORM_ALPHA = (2.0 * DEPTH) ** 0.25

COL_F = 0
COL_Q = COL_F + FOURIER_WIDTH
COL_K = COL_Q + RET_QK_WIDTH
COL_V = COL_K + RET_QK_WIDTH
COL_G = COL_V + RET_WIDTH
PROJ_WIDTH = COL_G + RET_WIDTH

V7X_LANES = 128
V7X_VMEM_LIMIT_BYTES = 56 * 1024 * 1024

TOKEN_TILE = 512
FFN_CHUNK = 256
DFT_RADIX = 4


def _dot(a, b):
    return jnp.dot(a, b, preferred_element_type=F32)


def _dot_tt(a, b, lhs_dim, rhs_dim):
    return lax.dot_general(a, b, (((lhs_dim,), (rhs_dim,)), ((), ())), preferred_element_type=F32)


def _layer_norm(x):
    mu = jnp.mean(x, axis=-1, keepdims=True)
    xc = x - mu
    var = jnp.mean(xc * xc, axis=-1, keepdims=True)
    return xc * lax.rsqrt(var + LN_EPS)


def _silu(x):
    return x * jax.nn.sigmoid(x)


def _log_sigmoid(x):
    return jnp.minimum(x, 0.0) - jnp.log1p(jnp.exp(-jnp.abs(x)))


@functools.lru_cache(maxsize=None)
def _rope_tables(n_tokens):
    nf = RET_QK_DIM // 4
    t = np.arange(n_tokens)
    row = (t // GRID_W).astype(np.float32)
    col = (t % GRID_W).astype(np.float32)
    freqs = (np.float32(ROPE_BASE) ** (-np.arange(nf, dtype=np.float32) / np.float32(nf))).astype(np.float32)
    ang_r = (row[:, None] * freqs[None, :]).astype(np.float32)
    ang_c = (col[:, None] * freqs[None, :]).astype(np.float32)
    cos_h = np.concatenate([np.cos(ang_r), np.cos(ang_r), np.cos(ang_c), np.cos(ang_c)], axis=1)
    sin_h = np.concatenate([-np.sin(ang_r), np.sin(ang_r), -np.sin(ang_c), np.sin(ang_c)], axis=1)
    reps = V7X_LANES // RET_QK_DIM
    return (np.tile(cos_h, (1, reps)).astype(np.float32), np.tile(sin_h, (1, reps)).astype(np.float32))


@functools.lru_cache(maxsize=None)
def _channel_dft_table():
    m = np.arange(FOURIER_GROUP_DIM)
    ang = 2.0 * np.pi * ((m[:, None] * m[None, :]) % FOURIER_GROUP_DIM) / FOURIER_GROUP_DIM
    scale = FOURIER_GROUP_DIM ** -0.5
    eye = np.eye(FOURIER_GROUPS)
    return np.concatenate([np.kron(eye, np.cos(ang) * scale), np.kron(eye, np.sin(ang) * scale)], axis=1)


@functools.lru_cache(maxsize=None)
def _position_dft_tables(n_tokens):
    sub = n_tokens // DFT_RADIX
    m = np.arange(sub, dtype=np.int64)
    ang = 2.0 * np.pi * ((m[:, None] * m[None, :]) % sub) / sub
    scale = n_tokens ** -0.5
    mat = np.concatenate([np.cos(ang) * scale, -np.sin(ang) * scale], axis=1)
    k1 = np.arange(1, DFT_RADIX, dtype=np.int64)
    phi = 2.0 * np.pi * ((k1[:, None] * m[None, :]) % n_tokens) / n_tokens
    lanes = np.ones((1, 1, V7X_LANES))
    return mat, np.cos(phi)[:, :, None] * lanes, np.sin(phi)[:, :, None] * lanes


def _modulation_kernel(c_ref, w_ref, b_ref, o_ref):
    s = _silu(c_ref[...]).astype(BF16)
    o_ref[...] = _dot(s, w_ref[...].astype(BF16)) + b_ref[...]


def _modulation(c_rows, w_mod, b_mod):
    rows, d = c_rows.shape
    width = w_mod.shape[1]
    tn = width // 4
    return pl.pallas_call(
        _modulation_kernel,
        grid=(width // tn,),
        in_specs=[
            pl.BlockSpec((rows, d), lambda j: (0, 0)),
            pl.BlockSpec((d, tn), lambda j: (0, j)),
            pl.BlockSpec((1, tn), lambda j: (0, j)),
        ],
        out_specs=pl.BlockSpec((rows, tn), lambda j: (0, j)),
        out_shape=jax.ShapeDtypeStruct((rows, width), F32),
        compiler_params=pltpu.CompilerParams(vmem_limit_bytes=V7X_VMEM_LIMIT_BYTES),
        name="modulation",
    )(c_rows, w_mod, b_mod.reshape(1, width))


def _block_diag_mask(shape):
    r = lax.broadcasted_iota(jnp.int32, shape, 0)
    c = lax.broadcasted_iota(jnp.int32, shape, 1)
    return (r >= RET_QK_DIM) == (c >= RET_V_DIM)


def _ctx_states_kernel(ctx_ref, sh_ref, sc_ref, w_ref, dfk_ref, dbk_ref, stf_ref, stb_ref):
    length = ctx_ref.shape[1]
    u = (_layer_norm(ctx_ref[0]) * (1.0 + sc_ref[...]) + sh_ref[...]).astype(BF16)
    k = _dot(u, w_ref[:, COL_K:COL_V]) * (RET_QK_DIM ** -0.5)
    v = _dot(u, w_ref[:, COL_V:COL_G]).astype(BF16)
    t = lax.broadcasted_iota(jnp.int32, (length, PAIR_QK), 0).astype(F32)
    bd = _block_diag_mask((PAIR_QK, PAIR_V))
    for p in range(HEAD_PAIRS):
        k2 = k[:, p * PAIR_QK:(p + 1) * PAIR_QK]
        v2 = v[:, p * PAIR_V:(p + 1) * PAIR_V]
        lg_f = _log_sigmoid(dfk_ref[p])
        lg_b = _log_sigmoid(dbk_ref[p])
        w_f = jnp.exp(lg_f * (length - 1.0 - t))
        w_b = jnp.exp(lg_b * t)
        st_f = _dot_tt((k2 * w_f).astype(BF16), v2, 0, 0)
        st_b = _dot_tt((k2 * w_b).astype(BF16), v2, 0, 0)
        stf_ref[0, p] = jnp.where(bd, st_f, 0.0)
        stb_ref[0, p] = jnp.where(bd, st_b, 0.0)


def _ctx_states(ctx, csh, csc, w_in_bf16, dec_f_k, dec_b_k):
    b, length, d = ctx.shape
    st_shape = jax.ShapeDtypeStruct((b, HEAD_PAIRS, PAIR_QK, PAIR_V), F32)
    st_spec = pl.BlockSpec((1, HEAD_PAIRS, PAIR_QK, PAIR_V), lambda i: (i, 0, 0, 0))
    dec_spec = pl.BlockSpec((HEAD_PAIRS, 1, PAIR_QK), lambda i: (0, 0, 0))
    return pl.pallas_call(
        _ctx_states_kernel,
        grid=(b,),
        in_specs=[
            pl.BlockSpec((1, length, d), lambda i: (i, 0, 0)),
            pl.BlockSpec((1, d), lambda i: (0, 0)),
            pl.BlockSpec((1, d), lambda i: (0, 0)),
            pl.BlockSpec((d, PROJ_WIDTH), lambda i: (0, 0)),
            dec_spec,
            dec_spec,
        ],
        out_specs=[st_spec, st_spec],
        out_shape=[st_shape, st_shape],
        compiler_params=pltpu.CompilerParams(vmem_limit_bytes=V7X_VMEM_LIMIT_BYTES),
        name="ctx_states",
    )(ctx, csh, csc, w_in_bf16, dec_f_k, dec_b_k)


def _rope(x, cos, sin_signed, first_of_pair):
    nf = RET_QK_DIM // 4
    outs = []
    for j in range(x.shape[1] // V7X_LANES):
        xj = x[:, j * V7X_LANES:(j + 1) * V7X_LANES]
        partner = jnp.where(first_of_pair, pltpu.roll(xj, V7X_LANES - nf, 1), pltpu.roll(xj, nf, 1))
        outs.append(xj * cos + partner * sin_signed)
    return jnp.concatenate(outs, axis=1)


def _tile_specs(tile_of_step, tiles_per_batch, tm, d, **kw):
    bi = lambda s: tile_of_step(s) // tiles_per_batch
    ti = lambda s: tile_of_step(s) % tiles_per_batch
    tok = lambda width: pl.BlockSpec((1, tm, width), lambda s: (bi(s), ti(s), 0), **kw)
    vec = pl.BlockSpec((1, 1, d), lambda s: (bi(s), 0, 0), **kw)
    four = pl.BlockSpec((tm, FOURIER_WIDTH), lambda s: (ti(s), bi(s)), **kw)
    rope = pl.BlockSpec((tm, V7X_LANES), lambda s: (ti(s), 0), **kw)
    return tok, vec, four, rope


def _in_proj_kernel(x0_ref, sh0_ref, sc0_ref, xn_ref, shn_ref, scn_ref, w_ref, wch_ref, cos_ref, sin_ref,
                    q_ref, k_ref, v_ref, g_ref, fc_ref, fs_ref, u_a, u_b):
    step = pl.program_id(0)

    def prepare(x_ref, sh_ref, sc_ref, u_dst):
        u_dst[...] = (_layer_norm(x_ref[0]) * (1.0 + sc_ref[0]) + sh_ref[0]).astype(BF16)

    def project(u_src):
        p = _dot(u_src[...], w_ref[...])
        f = p[:, COL_F:COL_Q].astype(BF16)
        fcs = _dot(f, wch_ref[...])
        fc_ref[...] = fcs[:, :FOURIER_WIDTH].astype(BF16)
        fs_ref[...] = fcs[:, FOURIER_WIDTH:].astype(BF16)

        cos = cos_ref[...]
        sin_signed = sin_ref[...]
        lane = lax.broadcasted_iota(jnp.int32, cos.shape, 1)
        first_of_pair = jnp.bitwise_and(lane, 2 * (RET_QK_DIM // 4) - 1) < (RET_QK_DIM // 4)
        q_ref[0] = _rope(p[:, COL_Q:COL_K], cos, sin_signed, first_of_pair).astype(BF16)
        k = p[:, COL_K:COL_V] * (RET_QK_DIM ** -0.5)
        k_ref[0] = _rope(k, cos, sin_signed, first_of_pair).astype(BF16)
        v_ref[0] = p[:, COL_V:COL_G].astype(BF16)
        g_ref[0] = _silu(p[:, COL_G:PROJ_WIDTH]).astype(BF16)

    @pl.when(step == 0)
    def _():
        prepare(x0_ref, sh0_ref, sc0_ref, u_a)

    even = lax.rem(step, 2) == 0

    @pl.when(even)
    def _():
        prepare(xn_ref, shn_ref, scn_ref, u_b)
        project(u_a)

    @pl.when(jnp.logical_not(even))
    def _():
        prepare(xn_ref, shn_ref, scn_ref, u_a)
        project(u_b)


def _in_proj(x, sh1, sc1, w_in_bf16, w_ch, cos_t, sin_t):
    b, n, d = x.shape
    tm = min(TOKEN_TILE, n)
    tiles_per_batch = n // tm
    n_tiles = b * tiles_per_batch
    tok0, vec0, _, _ = _tile_specs(lambda s: 0 * s, tiles_per_batch, tm, d, pipeline_mode=pl.Buffered(1))
    tokn, vecn, _, _ = _tile_specs(lambda s: jnp.minimum(s + 1, n_tiles - 1), tiles_per_batch, tm, d)
    tok, _, four, rope = _tile_specs(lambda s: s, tiles_per_batch, tm, d)
    resident = lambda shape: pl.BlockSpec(shape, lambda s: (0, 0), pipeline_mode=pl.Buffered(1))
    return pl.pallas_call(
        _in_proj_kernel,
        grid=(n_tiles,),
        in_specs=[
            tok0(d), vec0, vec0, tokn(d), vecn, vecn,
            resident((d, PROJ_WIDTH)), resident((FOURIER_WIDTH, 2 * FOURIER_WIDTH)),
            rope, rope,
        ],
        out_specs=[tok(RET_QK_WIDTH), tok(RET_QK_WIDTH), tok(RET_WIDTH), tok(RET_WIDTH), four, four],
        out_shape=[
            jax.ShapeDtypeStruct((b, n, RET_QK_WIDTH), BF16),
            jax.ShapeDtypeStruct((b, n, RET_QK_WIDTH), BF16),
            jax.ShapeDtypeStruct((b, n, RET_WIDTH), BF16),
            jax.ShapeDtypeStruct((b, n, RET_WIDTH), BF16),
            jax.ShapeDtypeStruct((n, b * FOURIER_WIDTH), BF16),
            jax.ShapeDtypeStruct((n, b * FOURIER_WIDTH), BF16),
        ],
        scratch_shapes=[pltpu.VMEM((tm, d), BF16), pltpu.VMEM((tm, d), BF16)],
        compiler_params=pltpu.CompilerParams(
            dimension_semantics=("arbitrary",), vmem_limit_bytes=V7X_VMEM_LIMIT_BYTES),
        name="in_proj",
    )(x, sh1, sc1, x, sh1, sc1, w_in_bf16, w_ch, cos_t, sin_t)


def _position_dft_kernel(fc_ref, fs_ref, mat_ref, twc_ref, tws_ref, o_ref):
    sub = mat_ref.shape[0]
    blk = lambda ref, j: ref[j * sub:(j + 1) * sub, :].astype(F32)
    c0, c1, c2, c3 = (blk(fc_ref, j) for j in range(DFT_RADIX))
    s0, s1, s2, s3 = (blk(fs_ref, j) for j in range(DFT_RADIX))
    tc0, tc1, tc2, tc3 = c0 + c2, c1 + c3, c0 - c2, c1 - c3
    ts0, ts1, ts2, ts3 = s0 + s2, s1 + s3, s0 - s2, s1 - s3
    parts = [
        (tc0 + tc1, ts0 + ts1),
        (tc2 - ts3, ts2 + tc3),
        (tc0 - tc1, ts0 - ts1),
        (tc2 + ts3, ts2 - tc3),
    ]
    mat = mat_ref[...].astype(BF16)
    reps = fc_ref.shape[1] // V7X_LANES
    for k1, (a, b) in enumerate(parts):
        if k1 > 0:
            cphi = jnp.concatenate([twc_ref[k1 - 1]] * reps, axis=1)
            sphi = jnp.concatenate([tws_ref[k1 - 1]] * reps, axis=1)
            a, b = a * cphi - b * sphi, a * sphi + b * cphi
        rhs = jnp.concatenate([a.astype(BF16), b.astype(BF16)], axis=0)
        y = _dot(mat, rhs)
        for h in range(reps):
            o_ref[0, h, pl.ds(k1, sub, stride=DFT_RADIX), :] = y[:, h * V7X_LANES:(h + 1) * V7X_LANES]


def _position_dft(mat, tw_cos, tw_sin, fc, fs, batch):
    n, cols = fc.shape
    width = cols // batch
    col = pl.BlockSpec((n, width), lambda bi: (0, bi))
    resident = lambda shape: pl.BlockSpec(shape, lambda bi: (0,) * len(shape), pipeline_mode=pl.Buffered(1))
    return pl.pallas_call(
        _position_dft_kernel,
        grid=(batch,),
        in_specs=[col, col, resident(mat.shape), resident(tw_cos.shape), resident(tw_sin.shape)],
        out_specs=pl.BlockSpec((1, width // V7X_LANES, n, V7X_LANES), lambda bi: (bi, 0, 0, 0)),
        out_shape=jax.ShapeDtypeStruct((batch, width // V7X_LANES, n, V7X_LANES), F32),
        compiler_params=pltpu.CompilerParams(vmem_limit_bytes=V7X_VMEM_LIMIT_BYTES),
        name="position_dft",
    )(fc, fs, mat, tw_cos, tw_sin)


def _retention_kernel(q_ref, k_ref, v_ref, g_ref, stf0_ref, stb0_ref, dfk_ref, dbk_ref, dfv_ref, dbv_ref,
                      o_ref, u_all, sf_all, sb_all, tab_s):
    n = q_ref.shape[1]
    n_chunks = n // RET_CHUNK
    c = RET_CHUNK

    lgf_k = _log_sigmoid(dfk_ref[0])
    lgb_k = _log_sigmoid(dbk_ref[0])
    lgf_v = _log_sigmoid(dfv_ref[0])
    lgb_v = _log_sigmoid(dbv_ref[0])

    i_k = lax.broadcasted_iota(jnp.int32, (c, PAIR_QK), 0).astype(F32)
    i_v = lax.broadcasted_iota(jnp.int32, (c, PAIR_V), 0)
    j_v = jnp.bitwise_and(lax.broadcasted_iota(jnp.int32, (c, PAIR_V), 1), RET_V_DIM - 1)
    diff = (i_v - j_v).astype(F32)
    bd = _block_diag_mask((PAIR_QK, PAIR_V))

    tab_s[0] = jnp.where(diff >= 0.0, jnp.exp(lgf_v * jnp.maximum(diff, 0.0)),
                         jnp.exp(lgb_v * jnp.maximum(-diff, 0.0)))
    i_vf = i_v.astype(F32)
    tab_s[1] = jnp.exp(lgf_v * (i_vf + 1.0))
    tab_s[2] = jnp.exp(lgb_v * (c - i_vf))
    g_f = jnp.exp(lgf_v * float(c))
    g_b = jnp.exp(lgb_v * float(c))
    zeta_f = jnp.exp(lgf_k * (c - 1.0 - i_k))
    zeta_b = jnp.exp(lgb_k * i_k)

    def chunk(ref, i):
        return ref[0, i * c:(i + 1) * c, :]

    for i in range(n_chunks):
        k2 = chunk(k_ref, i).astype(F32)
        kz = jnp.concatenate([(k2 * zeta_f).astype(BF16), (k2 * zeta_b).astype(BF16)], axis=1)
        u_all[i] = _dot_tt(kz, chunk(v_ref, i), 0, 0)

    s_f = stf0_ref[0, 0]
    s_b = stb0_ref[0, 0]
    for i in range(n_chunks):
        j = n_chunks - 1 - i
        sf_all[i] = jnp.where(bd, s_f, 0.0).astype(BF16)
        sb_all[j] = jnp.where(bd, s_b, 0.0).astype(BF16)
        if i + 1 < n_chunks:
            s_f = g_f * s_f + u_all[i, :PAIR_QK, :]
            s_b = g_b * s_b + u_all[j, PAIR_QK:, :]

    lane_k = lax.broadcasted_iota(jnp.int32, (c, PAIR_QK), 1)
    lane_v = lax.broadcasted_iota(jnp.int32, (c, PAIR_V), 1)
    zero_k = jnp.zeros((c, PAIR_QK), BF16)
    zero_v = jnp.zeros((c, PAIR_V), BF16)

    for i in range(n_chunks):
        q2 = chunk(q_ref, i)
        k2 = chunk(k_ref, i)
        v2 = chunk(v_ref, i)
        k_bd = jnp.concatenate([jnp.where(lane_k < RET_QK_DIM, k2, zero_k),
                                jnp.where(lane_k >= RET_QK_DIM, k2, zero_k)], axis=0)
        v_bd = jnp.concatenate([jnp.where(lane_v < RET_V_DIM, v2, zero_v),
                                jnp.where(lane_v >= RET_V_DIM, v2, zero_v)], axis=0)
        scores = _dot_tt(q2, k_bd, 1, 1)
        p = (scores * tab_s[0]).astype(BF16)
        y = _dot(p, v_bd)
        y = y + _dot(q2, sf_all[i]) * tab_s[1] + _dot(q2, sb_all[i]) * tab_s[2]
        g = chunk(g_ref, i).astype(F32)
        outs = []
        for h in range(2):
            yh = y[:, h * RET_V_DIM:(h + 1) * RET_V_DIM]
            ms = jnp.mean(yh * yh, axis=-1, keepdims=True)
            outs.append(yh * lax.rsqrt(ms + LN_EPS) * g[:, h * RET_V_DIM:(h + 1) * RET_V_DIM])
        o_ref[0, i * c:(i + 1) * c, :] = jnp.concatenate(outs, axis=1).astype(BF16)


def _retention(q, k, v, g, st_f, st_b, dec_f_k, dec_b_k, dec_f_v, dec_b_v):
    b, n, _ = q.shape
    n_chunks = n // RET_CHUNK
    qk = pl.BlockSpec((1, n, PAIR_QK), lambda bi, p: (bi, 0, p))
    vg = pl.BlockSpec((1, n, PAIR_V), lambda bi, p: (bi, 0, p))
    st = pl.BlockSpec((1, 1, PAIR_QK, PAIR_V), lambda bi, p: (bi, p, 0, 0))
    dk = pl.BlockSpec((1, 1, PAIR_QK), lambda bi, p: (p, 0, 0))
    dv = pl.BlockSpec((1, 1, PAIR_V), lambda bi, p: (p, 0, 0))
    return pl.pallas_call(
        _retention_kernel,
        grid=(b, HEAD_PAIRS),
        in_specs=[qk, qk, vg, vg, st, st, dk, dk, dv, dv],
        out_specs=vg,
        out_shape=jax.ShapeDtypeStruct((b, n, RET_WIDTH), BF16),
        scratch_shapes=[
            pltpu.VMEM((n_chunks, 2 * PAIR_QK, PAIR_V), F32),
            pltpu.VMEM((n_chunks, PAIR_QK, PAIR_V), BF16),
            pltpu.VMEM((n_chunks, PAIR_QK, PAIR_V), BF16),
            pltpu.VMEM((3, RET_CHUNK, PAIR_V), F32),
        ],
        compiler_params=pltpu.CompilerParams(vmem_limit_bytes=V7X_VMEM_LIMIT_BYTES),
        name="retention",
    )(q, k, v, g, st_f, st_b, dec_f_k, dec_b_k, dec_f_v, dec_b_v)


def _out_ffn_kernel(x0_ref, fm0_ref, yn0_ref, gt10_ref, sh20_ref, sc20_ref,
                    xn_ref, fmn_ref, ynn_ref, gt1n_ref, sh2n_ref, sc2n_ref,
                    gt2_ref, l1g_ref, l1b_ref, l2g_ref, l2b_ref, wo_ref, wg_ref, wu_ref, wd_ref,
                    o_ref, x1_a, x1_b, u2_a, u2_b):
    step = pl.program_id(0)

    def prepare(x_ref, fm_ref, yn_ref, gt1_ref, sh2_ref, sc2_ref, x1_dst, u2_dst):
        fm = jnp.concatenate([fm_ref[0, h] for h in range(fm_ref.shape[1])], axis=1).astype(BF16)
        mix = _dot(fm, wo_ref[:FOURIER_WIDTH, :]) + _dot(yn_ref[0], wo_ref[FOURIER_WIDTH:, :])
        x1 = _layer_norm(DEEPNORM_ALPHA * x_ref[0] + gt1_ref[0] * mix) * l1g_ref[...] + l1b_ref[...]
        x1_dst[...] = x1
        u2_dst[...] = (_layer_norm(x1) * (1.0 + sc2_ref[0]) + sh2_ref[0]).astype(BF16)

    def ffn(x1_src, u2_src):
        d_ff = wg_ref.shape[1]
        acc = None
        for c0 in range(0, d_ff, FFN_CHUNK):
            c1 = c0 + FFN_CHUNK
            u2 = u2_src[...]
            h = (_silu(_dot(u2, wg_ref[:, c0:c1])) * _dot(u2, wu_ref[:, c0:c1])).astype(BF16)
            part = _dot(h, wd_ref[c0:c1, :])
            acc = part if acc is None else acc + part
        z = DEEPNORM_ALPHA * x1_src[...] + gt2_ref[0] * acc
        o_ref[0] = _layer_norm(z) * l2g_ref[...] + l2b_ref[...]

    @pl.when(step == 0)
    def _():
        prepare(x0_ref, fm0_ref, yn0_ref, gt10_ref, sh20_ref, sc20_ref, x1_a, u2_a)

    even = lax.rem(step, 2) == 0

    @pl.when(even)
    def _():
        prepare(xn_ref, fmn_ref, ynn_ref, gt1n_ref, sh2n_ref, sc2n_ref, x1_b, u2_b)
        ffn(x1_a, u2_a)

    @pl.when(jnp.logical_not(even))
    def _():
        prepare(xn_ref, fmn_ref, ynn_ref, gt1n_ref, sh2n_ref, sc2n_ref, x1_a, u2_a)
        ffn(x1_b, u2_b)


def _out_ffn(x, fm, yn, gt1, sh2, sc2, gt2, l1g, l1b, l2g, l2b, wo, wg, wu, wd):
    b, n, d = x.shape
    d_ff = wg.shape[1]
    assert d_ff % FFN_CHUNK == 0
    tm = min(TOKEN_TILE, n)
    tiles_per_batch = n // tm
    n_tiles = b * tiles_per_batch
    planes = fm.shape[1]

    def specs(tile_of_step, **kw):
        tok, vec, _, _ = _tile_specs(tile_of_step, tiles_per_batch, tm, d, **kw)
        bi = lambda s: tile_of_step(s) // tiles_per_batch
        ti = lambda s: tile_of_step(s) % tiles_per_batch
        four = pl.BlockSpec((1, planes, tm, V7X_LANES), lambda s: (bi(s), 0, ti(s), 0), **kw)
        return tok, vec, four

    tok0, vec0, four0 = specs(lambda s: 0 * s, pipeline_mode=pl.Buffered(1))
    tokn, vecn, fourn = specs(lambda s: jnp.minimum(s + 1, n_tiles - 1))
    tokc, vecc, _ = specs(lambda s: s)
    par = pl.BlockSpec((1, d), lambda s: (0, 0))
    resident = lambda shape: pl.BlockSpec(shape, lambda s: (0, 0), pipeline_mode=pl.Buffered(1))
    return pl.pallas_call(
        _out_ffn_kernel,
        grid=(n_tiles,),
        in_specs=[
            tok0(d), four0, tok0(RET_WIDTH), vec0, vec0, vec0,
            tokn(d), fourn, tokn(RET_WIDTH), vecn, vecn, vecn,
            vecc, par, par, par, par,
            resident((d, d)), resident((d, d_ff)), resident((d, d_ff)), resident((d_ff, d)),
        ],
        out_specs=tokc(d),
        out_shape=jax.ShapeDtypeStruct((b, n, d), x.dtype),
        scratch_shapes=[
            pltpu.VMEM((tm, d), F32), pltpu.VMEM((tm, d), F32),
            pltpu.VMEM((tm, d), BF16), pltpu.VMEM((tm, d), BF16),
        ],
        compiler_params=pltpu.CompilerParams(
            dimension_semantics=("arbitrary",), vmem_limit_bytes=V7X_VMEM_LIMIT_BYTES),
        name="out_ffn",
    )(x, fm, yn, gt1, sh2, sc2, x, fm, yn, gt1, sh2, sc2, gt2, l1g, l1b, l2g, l2b, wo, wg, wu, wd)


def _pair_lanes(decay, width):
    return jnp.repeat(decay.astype(F32), width).reshape(HEAD_PAIRS, 1, 2 * width)


def kernel(x, c, ctx, c_ctx, w_mod, b_mod, w_in, w_out, decay_fwd, decay_bwd, ln1_g, ln1_b,
           w_ffn_gate, w_ffn_up, w_ffn_down, ln2_g, ln2_b):
    b, n, d = x.shape
    assert w_mod.shape[0] == DEPTH and n % RET_CHUNK == 0 and n % GRID_W == 0 and n % (8 * DFT_RADIX) == 0
    assert ctx.shape[1] % RET_CHUNK == 0 and w_in.shape[2] == PROJ_WIDTH

    rows = -(-(b + 1) // 8) * 8
    c_rows = jnp.zeros((rows, d), F32).at[:b].set(c).at[b].set(c_ctx)
    mod = _modulation(c_rows, w_mod[0], b_mod[0])
    sh1, sc1, gt1, sh2, sc2, gt2 = [mod[:b, i * d:(i + 1) * d].reshape(b, 1, d) for i in range(6)]
    csh1 = mod[b:b + 1, 0:d]
    csc1 = mod[b:b + 1, d:2 * d]

    dec_f_k = _pair_lanes(decay_fwd[0], RET_QK_DIM)
    dec_b_k = _pair_lanes(decay_bwd[0], RET_QK_DIM)
    dec_f_v = _pair_lanes(decay_fwd[0], RET_V_DIM)
    dec_b_v = _pair_lanes(decay_bwd[0], RET_V_DIM)

    w_in_bf16 = w_in[0].astype(BF16)
    st_f, st_b = _ctx_states(ctx, csh1, csc1, w_in_bf16, dec_f_k, dec_b_k)

    cos_t, sin_t = _rope_tables(n)
    w_ch = jnp.asarray(_channel_dft_table(), dtype=F32).astype(BF16)
    q, k, v, g, fc, fs = _in_proj(x, sh1, sc1, w_in_bf16, w_ch, jnp.asarray(cos_t), jnp.asarray(sin_t))

    dft_mat, tw_cos, tw_sin = (jnp.asarray(t, dtype=F32) for t in _position_dft_tables(n))
    fm = _position_dft(dft_mat, tw_cos, tw_sin, fc, fs, b)

    yn = _retention(q, k, v, g, st_f, st_b, dec_f_k, dec_b_k, dec_f_v, dec_b_v)

    return _out_ffn(x, fm, yn, gt1, sh2, sc2, gt2,
                    ln1_g[0].reshape(1, d), ln1_b[0].reshape(1, d), ln2_g[0].reshape(1, d), ln2_b[0].reshape(1, d),
                    w_out[0].astype(BF16), w_ffn_gate[0].astype(BF16), w_ffn_up[0].astype(BF16),
                    w_ffn_down[0].astype(BF16))
```

```python
import functools

import numpy as np
import jax
import jax.numpy as jnp
from jax import lax
from jax.experimental import pallas as pl
from jax.experimental.pallas import tpu as pltpu

F32 = jnp.float32
BF16 = jnp.bfloat16

GRID_W = 64
FOURIER_GROUPS = 4
FOURIER_GROUP_DIM = 64
FOURIER_WIDTH = FOURIER_GROUPS * FOURIER_GROUP_DIM
RET_HEADS = 6
RET_QK_DIM = 64
RET_V_DIM = 128
RET_QK_WIDTH = RET_HEADS * RET_QK_DIM
RET_WIDTH = RET_HEADS * RET_V_DIM
RET_CHUNK = 128
HEAD_PAIRS = RET_HEADS // 2
PAIR_QK = 2 * RET_QK_DIM
PAIR_V = 2 * RET_V_DIM
ROPE_BASE = 10000.0
LN_EPS = 1e-6
DEPTH = 1
DEEPNORM_ALPHA = (2.0 * DEPTH) ** 0.25

COL_F = 0
COL_Q = COL_F + FOURIER_WIDTH
COL_K = COL_Q + RET_QK_WIDTH
COL_V = COL_K + RET_QK_WIDTH
COL_G = COL_V + RET_WIDTH
PROJ_WIDTH = COL_G + RET_WIDTH

V7X_LANES = 128
V7X_VMEM_LIMIT_BYTES = 56 * 1024 * 1024

TOKEN_TILE = 512
FFN_CHUNK = 256
DFT_RADIX = 4
CTX_BATCH_GROUP = 4
BF16_SUBLANES = 16


def _dot(a, b):
    return jnp.dot(a, b, preferred_element_type=F32)


def _dot_tt(a, b, lhs_dim, rhs_dim):
    return lax.dot_general(a, b, (((lhs_dim,), (rhs_dim,)), ((), ())), preferred_element_type=F32)


def _layer_norm(x):
    mu = jnp.mean(x, axis=-1, keepdims=True)
    xc = x - mu
    var = jnp.mean(xc * xc, axis=-1, keepdims=True)
    return xc * lax.rsqrt(var + LN_EPS)


def _silu(x):
    return x * jax.nn.sigmoid(x)


def _log_sigmoid(x):
    return jnp.minimum(x, 0.0) - jnp.log1p(jnp.exp(-jnp.abs(x)))


@functools.lru_cache(maxsize=None)
def _rope_tables(n_tokens):
    nf = RET_QK_DIM // 4
    t = np.arange(n_tokens)
    row = (t // GRID_W).astype(np.float32)
    col = (t % GRID_W).astype(np.float32)
    freqs = (np.float32(ROPE_BASE) ** (-np.arange(nf, dtype=np.float32) / np.float32(nf))).astype(np.float32)
    ang_r = (row[:, None] * freqs[None, :]).astype(np.float32)
    ang_c = (col[:, None] * freqs[None, :]).astype(np.float32)
    cos_h = np.concatenate([np.cos(ang_r), np.cos(ang_r), np.cos(ang_c), np.cos(ang_c)], axis=1)
    sin_h = np.concatenate([-np.sin(ang_r), np.sin(ang_r), -np.sin(ang_c), np.sin(ang_c)], axis=1)
    reps = V7X_LANES // RET_QK_DIM
    return (np.tile(cos_h, (1, reps)).astype(np.float32), np.tile(sin_h, (1, reps)).astype(np.float32))


@functools.lru_cache(maxsize=None)
def _channel_dft_table():
    m = np.arange(FOURIER_GROUP_DIM)
    ang = 2.0 * np.pi * ((m[:, None] * m[None, :]) % FOURIER_GROUP_DIM) / FOURIER_GROUP_DIM
    scale = FOURIER_GROUP_DIM ** -0.5
    eye = np.eye(FOURIER_GROUPS)
    return np.concatenate([np.kron(eye, np.cos(ang) * scale), np.kron(eye, np.sin(ang) * scale)], axis=1)


@functools.lru_cache(maxsize=None)
def _position_dft_tables(n_tokens):
    sub = n_tokens // DFT_RADIX
    m = np.arange(sub, dtype=np.int64)
    ang = 2.0 * np.pi * ((m[:, None] * m[None, :]) % sub) / sub
    scale = n_tokens ** -0.5
    mat = np.concatenate([np.cos(ang) * scale, -np.sin(ang) * scale], axis=1)
    k1 = np.arange(1, DFT_RADIX, dtype=np.int64)
    phi = 2.0 * np.pi * ((k1[:, None] * m[None, :]) % n_tokens) / n_tokens
    lanes = np.ones((1, 1, V7X_LANES))
    return mat, np.cos(phi)[:, :, None] * lanes, np.sin(phi)[:, :, None] * lanes


def _modulation_kernel(c_ref, w_ref, b_ref, o_ref):
    s = _silu(c_ref[...]).astype(BF16)
    o_ref[...] = _dot(s, w_ref[...].astype(BF16)) + b_ref[...]


def _modulation(c_rows, w_mod, b_mod):
    rows, d = c_rows.shape
    width = w_mod.shape[1]
    tn = width // 4
    return pl.pallas_call(
        _modulation_kernel,
        grid=(width // tn,),
        in_specs=[
            pl.BlockSpec((rows, d), lambda j: (0, 0)),
            pl.BlockSpec((d, tn), lambda j: (0, j)),
            pl.BlockSpec((1, tn), lambda j: (0, j)),
        ],
        out_specs=pl.BlockSpec((rows, tn), lambda j: (0, j)),
        out_shape=jax.ShapeDtypeStruct((rows, width), F32),
        compiler_params=pltpu.CompilerParams(vmem_limit_bytes=V7X_VMEM_LIMIT_BYTES),
        name="modulation",
    )(c_rows, w_mod, b_mod.reshape(1, width))


def _block_diag_mask(shape):
    r = lax.broadcasted_iota(jnp.int32, shape, 0)
    c = lax.broadcasted_iota(jnp.int32, shape, 1)
    return (r >= RET_QK_DIM) == (c >= RET_V_DIM)


def _ctx_states_kernel(ctx_ref, sh_ref, sc_ref, w_ref, dfk_ref, dbk_ref, stf_ref, stb_ref):
    group, length, d = ctx_ref.shape
    x = ctx_ref[...].reshape(group * length, d)
    u = (_layer_norm(x) * (1.0 + sc_ref[...]) + sh_ref[...]).astype(BF16)
    kv = _dot(u, w_ref[:, COL_K:COL_G].astype(BF16))
    k = kv[:, :RET_QK_WIDTH] * (RET_QK_DIM ** -0.5)
    v = kv[:, RET_QK_WIDTH:].astype(BF16)
    t = lax.broadcasted_iota(jnp.int32, (length, PAIR_QK), 0).astype(F32)
    bd = _block_diag_mask((PAIR_QK, PAIR_V))
    for p in range(HEAD_PAIRS):
        w_f = jnp.exp(_log_sigmoid(dfk_ref[p]) * (length - 1.0 - t))
        w_b = jnp.exp(_log_sigmoid(dbk_ref[p]) * t)
        for bi in range(group):
            rows = slice(bi * length, (bi + 1) * length)
            k2 = k[rows, p * PAIR_QK:(p + 1) * PAIR_QK]
            v2 = v[rows, p * PAIR_V:(p + 1) * PAIR_V]
            st_f = _dot_tt((k2 * w_f).astype(BF16), v2, 0, 0)
            st_b = _dot_tt((k2 * w_b).astype(BF16), v2, 0, 0)
            stf_ref[bi, p] = jnp.where(bd, st_f, 0.0)
            stb_ref[bi, p] = jnp.where(bd, st_b, 0.0)


def _ctx_states(ctx, csh, csc, w_in, dec_f_k, dec_b_k):
    b, length, d = ctx.shape
    group = CTX_BATCH_GROUP if b % CTX_BATCH_GROUP == 0 else 1
    st_shape = jax.ShapeDtypeStruct((b, HEAD_PAIRS, PAIR_QK, PAIR_V), F32)
    st_spec = pl.BlockSpec((group, HEAD_PAIRS, PAIR_QK, PAIR_V), lambda i: (i, 0, 0, 0))
    dec_spec = pl.BlockSpec((HEAD_PAIRS, 1, PAIR_QK), lambda i: (0, 0, 0))
    return pl.pallas_call(
        _ctx_states_kernel,
        grid=(b // group,),
        in_specs=[
            pl.BlockSpec((group, length, d), lambda i: (i, 0, 0)),
            pl.BlockSpec((1, d), lambda i: (0, 0)),
            pl.BlockSpec((1, d), lambda i: (0, 0)),
            pl.BlockSpec((d, PROJ_WIDTH), lambda i: (0, 0), pipeline_mode=pl.Buffered(1)),
            dec_spec,
            dec_spec,
        ],
        out_specs=[st_spec, st_spec],
        out_shape=[st_shape, st_shape],
        compiler_params=pltpu.CompilerParams(vmem_limit_bytes=V7X_VMEM_LIMIT_BYTES),
        name="ctx_states",
    )(ctx, csh, csc, w_in, dec_f_k, dec_b_k)


def _rope(x, cos, sin_signed, first_of_pair):
    nf = RET_QK_DIM // 4
    outs = []
    for j in range(x.shape[1] // V7X_LANES):
        xj = x[:, j * V7X_LANES:(j + 1) * V7X_LANES]
        partner = jnp.where(first_of_pair, pltpu.roll(xj, V7X_LANES - nf, 1), pltpu.roll(xj, nf, 1))
        outs.append(xj * cos + partner * sin_signed)
    return jnp.concatenate(outs, axis=1)


def _tile_specs(tile_of_step, tiles_per_batch, tm, d, **kw):
    bi = lambda s: tile_of_step(s) // tiles_per_batch
    ti = lambda s: tile_of_step(s) % tiles_per_batch
    tok = lambda width: pl.BlockSpec((1, tm, width), lambda s: (bi(s), ti(s), 0), **kw)
    vec = pl.BlockSpec((1, 1, d), lambda s: (bi(s), 0, 0), **kw)
    four = pl.BlockSpec((tm, FOURIER_WIDTH), lambda s: (ti(s), bi(s)), **kw)
    rope = pl.BlockSpec((tm, V7X_LANES), lambda s: (ti(s), 0), **kw)
    return tok, vec, four, rope


def _in_proj_kernel(x0_ref, sh0_ref, sc0_ref, xn_ref, shn_ref, scn_ref, w_ref, wch_ref, cos_ref, sin_ref,
                    wo_ref, wg_ref, wu_ref, wd_ref,
                    qk_ref, vg_ref, fcs_ref, wo_out, wg_out, wu_out, wd_out, u_a, u_b, w_s):
    step = pl.program_id(0)
    for src, dst in ((wo_ref, wo_out), (wg_ref, wg_out), (wu_ref, wu_out), (wd_ref, wd_out)):
        dst[...] = src[...].astype(BF16)

    def prepare(x_ref, sh_ref, sc_ref, u_dst):
        u_dst[...] = (_layer_norm(x_ref[0]) * (1.0 + sc_ref[0]) + sh_ref[0]).astype(BF16)

    def project(u_src):
        p = _dot(u_src[...], w_s[...])
        f = p[:, COL_F:COL_Q].astype(BF16)
        fcs_ref[0] = _dot(f, wch_ref[...]).astype(BF16)

        cos = cos_ref[...]
        sin_signed = sin_ref[...]
        lane = lax.broadcasted_iota(jnp.int32, cos.shape, 1)
        first_of_pair = jnp.bitwise_and(lane, 2 * (RET_QK_DIM // 4) - 1) < (RET_QK_DIM // 4)
        qk_ref[0, :, :RET_QK_WIDTH] = _rope(p[:, COL_Q:COL_K], cos, sin_signed, first_of_pair).astype(BF16)
        k = p[:, COL_K:COL_V] * (RET_QK_DIM ** -0.5)
        qk_ref[0, :, RET_QK_WIDTH:] = _rope(k, cos, sin_signed, first_of_pair).astype(BF16)
        vg_ref[0, :, :RET_WIDTH] = p[:, COL_V:COL_G].astype(BF16)
        vg_ref[0, :, RET_WIDTH:] = _silu(p[:, COL_G:PROJ_WIDTH]).astype(BF16)

    @pl.when(step == 0)
    def _():
        w_s[...] = w_ref[...].astype(BF16)
        prepare(x0_ref, sh0_ref, sc0_ref, u_a)

    even = lax.rem(step, 2) == 0

    @pl.when(even)
    def _():
        prepare(xn_ref, shn_ref, scn_ref, u_b)
        project(u_a)

    @pl.when(jnp.logical_not(even))
    def _():
        prepare(xn_ref, shn_ref, scn_ref, u_a)
        project(u_b)


def _cast_block_rows(rows, n_steps):
    for r in range(BF16_SUBLANES, rows + 1, BF16_SUBLANES):
        if rows % r == 0 and rows // r <= n_steps:
            return r
    raise ValueError("no aligned row blocking for the weight cast")


def _in_proj(x, sh1, sc1, w_in, w_ch, cos_t, sin_t, ffn_weights):
    b, n, d = x.shape
    tm = min(TOKEN_TILE, n)
    tiles_per_batch = n // tm
    n_tiles = b * tiles_per_batch
    tok0, vec0, _, _ = _tile_specs(lambda s: 0 * s, tiles_per_batch, tm, d, pipeline_mode=pl.Buffered(1))
    tokn, vecn, _, _ = _tile_specs(lambda s: jnp.minimum(s + 1, n_tiles - 1), tiles_per_batch, tm, d)
    tok, _, _, rope = _tile_specs(lambda s: s, tiles_per_batch, tm, d)
    resident = lambda shape: pl.BlockSpec(shape, lambda s: (0, 0), pipeline_mode=pl.Buffered(1))

    cast_specs = []
    for w in ffn_weights:
        r = _cast_block_rows(w.shape[0], n_tiles)
        last = w.shape[0] // r - 1
        cast_specs.append(pl.BlockSpec((r, w.shape[1]), lambda s, last=last: (jnp.minimum(s, last), 0)))

    widths = (2 * RET_QK_WIDTH, 2 * RET_WIDTH, 2 * FOURIER_WIDTH)
    return pl.pallas_call(
        _in_proj_kernel,
        grid=(n_tiles,),
        in_specs=[
            tok0(d), vec0, vec0, tokn(d), vecn, vecn,
            resident((d, PROJ_WIDTH)), resident((FOURIER_WIDTH, 2 * FOURIER_WIDTH)),
            rope, rope, *cast_specs,
        ],
        out_specs=[tok(width) for width in widths] + cast_specs,
        out_shape=[jax.ShapeDtypeStruct((b, n, width), BF16) for width in widths]
        + [jax.ShapeDtypeStruct(w.shape, BF16) for w in ffn_weights],
        scratch_shapes=[pltpu.VMEM((tm, d), BF16), pltpu.VMEM((tm, d), BF16), pltpu.VMEM((d, PROJ_WIDTH), BF16)],
        compiler_params=pltpu.CompilerParams(
            dimension_semantics=("arbitrary",), vmem_limit_bytes=V7X_VMEM_LIMIT_BYTES),
        name="in_proj",
    )(x, sh1, sc1, x, sh1, sc1, w_in, w_ch, cos_t, sin_t, *ffn_weights)


def _position_dft_kernel(fcs_ref, mat_ref, twc_ref, tws_ref, o_ref):
    sub = mat_ref.shape[0]
    width = fcs_ref.shape[2] // 2
    c0, c1, c2, c3 = (fcs_ref[0, j * sub:(j + 1) * sub, :width].astype(F32) for j in range(DFT_RADIX))
    s0, s1, s2, s3 = (fcs_ref[0, j * sub:(j + 1) * sub, width:].astype(F32) for j in range(DFT_RADIX))
    tc0, tc1, tc2, tc3 = c0 + c2, c1 + c3, c0 - c2, c1 - c3
    ts0, ts1, ts2, ts3 = s0 + s2, s1 + s3, s0 - s2, s1 - s3
    parts = [
        (tc0 + tc1, ts0 + ts1),
        (tc2 - ts3, ts2 + tc3),
        (tc0 - tc1, ts0 - ts1),
        (tc2 + ts3, ts2 - tc3),
    ]
    mat = mat_ref[...].astype(BF16)
    reps = width // V7X_LANES
    for k1, (a, b) in enumerate(parts):
        if k1 > 0:
            cphi = jnp.concatenate([twc_ref[k1 - 1]] * reps, axis=1)
            sphi = jnp.concatenate([tws_ref[k1 - 1]] * reps, axis=1)
            a, b = a * cphi - b * sphi, a * sphi + b * cphi
        rhs = jnp.concatenate([a.astype(BF16), b.astype(BF16)], axis=0)
        y = _dot(mat, rhs)
        for h in range(reps):
            o_ref[0, h, pl.ds(k1, sub, stride=DFT_RADIX), :] = y[:, h * V7X_LANES:(h + 1) * V7X_LANES]


def _position_dft(mat, tw_cos, tw_sin, fcs):
    batch, n, width2 = fcs.shape
    width = width2 // 2
    resident = lambda shape: pl.BlockSpec(shape, lambda bi: (0,) * len(shape), pipeline_mode=pl.Buffered(1))
    return pl.pallas_call(
        _position_dft_kernel,
        grid=(batch,),
        in_specs=[pl.BlockSpec((1, n, width2), lambda bi: (bi, 0, 0)),
                  resident(mat.shape), resident(tw_cos.shape), resident(tw_sin.shape)],
        out_specs=pl.BlockSpec((1, width // V7X_LANES, n, V7X_LANES), lambda bi: (bi, 0, 0, 0)),
        out_shape=jax.ShapeDtypeStruct((batch, width // V7X_LANES, n, V7X_LANES), F32),
        compiler_params=pltpu.CompilerParams(vmem_limit_bytes=V7X_VMEM_LIMIT_BYTES),
        name="position_dft",
    )(fcs, mat, tw_cos, tw_sin)


def _retention_kernel(q_ref, k_ref, v_ref, g_ref, stf0_ref, stb0_ref, dfk_ref, dbk_ref, dfv_ref, dbv_ref,
                      o_ref, u_all, s_all, tab_s, xi_s):
    n = q_ref.shape[1]
    n_chunks = n // RET_CHUNK
    c = RET_CHUNK

    lgf_k = _log_sigmoid(dfk_ref[0])
    lgb_k = _log_sigmoid(dbk_ref[0])
    lgf_v = _log_sigmoid(dfv_ref[0])
    lgb_v = _log_sigmoid(dbv_ref[0])

    i_k = lax.broadcasted_iota(jnp.int32, (c, PAIR_QK), 0).astype(F32)
    i_v = lax.broadcasted_iota(jnp.int32, (c, PAIR_V), 0)
    j_v = jnp.bitwise_and(lax.broadcasted_iota(jnp.int32, (c, PAIR_V), 1), RET_V_DIM - 1)
    diff = (i_v - j_v).astype(F32)
    bd = _block_diag_mask((PAIR_QK, PAIR_V))

    tab_s[...] = jnp.where(diff >= 0.0, jnp.exp(lgf_v * jnp.maximum(diff, 0.0)),
                           jnp.exp(lgb_v * jnp.maximum(-diff, 0.0)))
    xi_s[0] = jnp.exp(lgf_k * (i_k + 1.0))
    xi_s[1] = jnp.exp(lgb_k * (c - i_k))
    g_f = jnp.exp(lgf_v * float(c))
    g_b = jnp.exp(lgb_v * float(c))
    zeta_f = jnp.exp(lgf_k * (c - 1.0 - i_k))
    zeta_b = jnp.exp(lgb_k * i_k)

    def chunk(ref, i):
        return ref[0, i * c:(i + 1) * c, :]

    for i in range(n_chunks):
        k2 = chunk(k_ref, i).astype(F32)
        kz = jnp.concatenate([(k2 * zeta_f).astype(BF16), (k2 * zeta_b).astype(BF16)], axis=1)
        u_all[i] = _dot_tt(kz, chunk(v_ref, i), 0, 0)

    s_f = stf0_ref[0, 0]
    s_b = stb0_ref[0, 0]
    for i in range(n_chunks):
        j = n_chunks - 1 - i
        s_all[i, :PAIR_QK, :] = jnp.where(bd, s_f, 0.0).astype(BF16)
        s_all[j, PAIR_QK:, :] = jnp.where(bd, s_b, 0.0).astype(BF16)
        if i + 1 < n_chunks:
            s_f = g_f * s_f + u_all[i, :PAIR_QK, :]
            s_b = g_b * s_b + u_all[j, PAIR_QK:, :]

    lane_k = lax.broadcasted_iota(jnp.int32, (c, PAIR_QK), 1)
    lane_v = lax.broadcasted_iota(jnp.int32, (c, PAIR_V), 1)
    zero_k = jnp.zeros((c, PAIR_QK), BF16)
    zero_v = jnp.zeros((c, PAIR_V), BF16)

    for i in range(n_chunks):
        q2 = chunk(q_ref, i)
        k2 = chunk(k_ref, i)
        v2 = chunk(v_ref, i)
        k_bd = jnp.concatenate([jnp.where(lane_k < RET_QK_DIM, k2, zero_k),
                                jnp.where(lane_k >= RET_QK_DIM, k2, zero_k)], axis=0)
        v_bd = jnp.concatenate([jnp.where(lane_v < RET_V_DIM, v2, zero_v),
                                jnp.where(lane_v >= RET_V_DIM, v2, zero_v)], axis=0)
        scores = _dot_tt(q2, k_bd, 1, 1)
        p = (scores * tab_s[...]).astype(BF16)
        q2f = q2.astype(F32)
        qx = jnp.concatenate([(q2f * xi_s[0]).astype(BF16), (q2f * xi_s[1]).astype(BF16)], axis=1)
        y = _dot(p, v_bd) + _dot(qx, s_all[i])
        g = chunk(g_ref, i).astype(F32)
        outs = []
        for h in range(2):
            yh = y[:, h * RET_V_DIM:(h + 1) * RET_V_DIM]
            ms = jnp.mean(yh * yh, axis=-1, keepdims=True)
            outs.append(yh * lax.rsqrt(ms + LN_EPS) * g[:, h * RET_V_DIM:(h + 1) * RET_V_DIM])
        o_ref[0, i * c:(i + 1) * c, :] = jnp.concatenate(outs, axis=1).astype(BF16)


def _retention(qk, vg, st_f, st_b, dec_f_k, dec_b_k, dec_f_v, dec_b_v):
    b, n, _ = qk.shape
    n_chunks = n // RET_CHUNK
    slab = lambda width, first: pl.BlockSpec((1, n, width), lambda bi, p: (bi, 0, first + p))
    st = pl.BlockSpec((1, 1, PAIR_QK, PAIR_V), lambda bi, p: (bi, p, 0, 0))
    dk = pl.BlockSpec((1, 1, PAIR_QK), lambda bi, p: (p, 0, 0))
    dv = pl.BlockSpec((1, 1, PAIR_V), lambda bi, p: (p, 0, 0))
    return pl.pallas_call(
        _retention_kernel,
        grid=(b, HEAD_PAIRS),
        in_specs=[slab(PAIR_QK, 0), slab(PAIR_QK, HEAD_PAIRS), slab(PAIR_V, 0), slab(PAIR_V, HEAD_PAIRS),
                  st, st, dk, dk, dv, dv],
        out_specs=slab(PAIR_V, 0),
        out_shape=jax.ShapeDtypeStruct((b, n, RET_WIDTH), BF16),
        scratch_shapes=[
            pltpu.VMEM((n_chunks, 2 * PAIR_QK, PAIR_V), F32),
            pltpu.VMEM((n_chunks, 2 * PAIR_QK, PAIR_V), BF16),
            pltpu.VMEM((RET_CHUNK, PAIR_V), F32),
            pltpu.VMEM((2, RET_CHUNK, PAIR_QK), F32),
        ],
        compiler_params=pltpu.CompilerParams(vmem_limit_bytes=V7X_VMEM_LIMIT_BYTES),
        name="retention",
    )(qk, qk, vg, vg, st_f, st_b, dec_f_k, dec_b_k, dec_f_v, dec_b_v)


def _out_ffn_kernel(x0_ref, fm0_ref, yn0_ref, gt10_ref, sh20_ref, sc20_ref,
                    xn_ref, fmn_ref, ynn_ref, gt1n_ref, sh2n_ref, sc2n_ref,
                    gt2_ref, l1g_ref, l1b_ref, l2g_ref, l2b_ref, wo_ref, wg_ref, wu_ref, wd_ref,
                    o_ref, x1_a, x1_b, u2_a, u2_b):
    step = pl.program_id(0)

    def prepare(x_ref, fm_ref, yn_ref, gt1_ref, sh2_ref, sc2_ref, x1_dst, u2_dst):
        fm = jnp.concatenate([fm_ref[0, h] for h in range(fm_ref.shape[1])], axis=1).astype(BF16)
        mix = _dot(fm, wo_ref[:FOURIER_WIDTH, :]) + _dot(yn_ref[0], wo_ref[FOURIER_WIDTH:, :])
        x1 = _layer_norm(DEEPNORM_ALPHA * x_ref[0] + gt1_ref[0] * mix) * l1g_ref[...] + l1b_ref[...]
        x1_dst[...] = x1
        u2_dst[...] = (_layer_norm(x1) * (1.0 + sc2_ref[0]) + sh2_ref[0]).astype(BF16)

    def ffn(x1_src, u2_src):
        d_ff = wg_ref.shape[1]
        acc = None
        for c0 in range(0, d_ff, FFN_CHUNK):
            c1 = c0 + FFN_CHUNK
            u2 = u2_src[...]
            h = (_silu(_dot(u2, wg_ref[:, c0:c1])) * _dot(u2, wu_ref[:, c0:c1])).astype(BF16)
            part = _dot(h, wd_ref[c0:c1, :])
            acc = part if acc is None else acc + part
        z = DEEPNORM_ALPHA * x1_src[...] + gt2_ref[0] * acc
        o_ref[0] = _layer_norm(z) * l2g_ref[...] + l2b_ref[...]

    @pl.when(step == 0)
    def _():
        prepare(x0_ref, fm0_ref, yn0_ref, gt10_ref, sh20_ref, sc20_ref, x1_a, u2_a)

    even = lax.rem(step, 2) == 0

    @pl.when(even)
    def _():
        prepare(xn_ref, fmn_ref, ynn_ref, gt1n_ref, sh2n_ref, sc2n_ref, x1_b, u2_b)
        ffn(x1_a, u2_a)

    @pl.when(jnp.logical_not(even))
    def _():
        prepare(xn_ref, fmn_ref, ynn_ref, gt1n_ref, sh2n_ref, sc2n_ref, x1_a, u2_a)
        ffn(x1_b, u2_b)


def _out_ffn(x, fm, yn, gt1, sh2, sc2, gt2, l1g, l1b, l2g, l2b, wo, wg, wu, wd):
    b, n, d = x.shape
    d_ff = wg.shape[1]
    assert d_ff % FFN_CHUNK == 0
    tm = min(TOKEN_TILE, n)
    tiles_per_batch = n // tm
    n_tiles = b * tiles_per_batch
    planes = fm.shape[1]

    def specs(tile_of_step, **kw):
        tok, vec, _, _ = _tile_specs(tile_of_step, tiles_per_batch, tm, d, **kw)
        bi = lambda s: tile_of_step(s) // tiles_per_batch
        ti = lambda s: tile_of_step(s) % tiles_per_batch
        four = pl.BlockSpec((1, planes, tm, V7X_LANES), lambda s: (bi(s), 0, ti(s), 0), **kw)
        return tok, vec, four

    tok0, vec0, four0 = specs(lambda s: 0 * s, pipeline_mode=pl.Buffered(1))
    tokn, vecn, fourn = specs(lambda s: jnp.minimum(s + 1, n_tiles - 1))
    tokc, vecc, _ = specs(lambda s: s)
    par = pl.BlockSpec((1, d), lambda s: (0, 0))
    resident = lambda shape: pl.BlockSpec(shape, lambda s: (0, 0), pipeline_mode=pl.Buffered(1))
    return pl.pallas_call(
        _out_ffn_kernel,
        grid=(n_tiles,),
        in_specs=[
            tok0(d), four0, tok0(RET_WIDTH), vec0, vec0, vec0,
            tokn(d), fourn, tokn(RET_WIDTH), vecn, vecn, vecn,
            vecc, par, par, par, par,
            resident((d, d)), resident((d, d_ff)), resident((d, d_ff)), resident((d_ff, d)),
        ],
        out_specs=tokc(d),
        out_shape=jax.ShapeDtypeStruct((b, n, d), x.dtype),
        scratch_shapes=[
            pltpu.VMEM((tm, d), F32), pltpu.VMEM((tm, d), F32),
            pltpu.VMEM((tm, d), BF16), pltpu.VMEM((tm, d), BF16),
        ],
        compiler_params=pltpu.CompilerParams(
            dimension_semantics=("arbitrary",), vmem_limit_bytes=V7X_VMEM_LIMIT_BYTES),
        name="out_ffn",
    )(x, fm, yn, gt1, sh2, sc2, x, fm, yn, gt1, sh2, sc2, gt2, l1g, l1b, l2g, l2b, wo, wg, wu, wd)


def _pair_lanes(decay, width):
    return jnp.repeat(decay.astype(F32), width).reshape(HEAD_PAIRS, 1, 2 * width)


def kernel(x, c, ctx, c_ctx, w_mod, b_mod, w_in, w_out, decay_fwd, decay_bwd, ln1_g, ln1_b,
           w_ffn_gate, w_ffn_up, w_ffn_down, ln2_g, ln2_b):
    b, n, d = x.shape
    assert w_mod.shape[0] == DEPTH and n % RET_CHUNK == 0 and n % GRID_W == 0 and n % (8 * DFT_RADIX) == 0
    assert ctx.shape[1] % RET_CHUNK == 0 and w_in.shape[2] == PROJ_WIDTH

    rows = -(-(b + 1) // 8) * 8
    c_rows = jnp.zeros((rows, d), F32).at[:b].set(c).at[b].set(c_ctx)
    mod = _modulation(c_rows, w_mod[0], b_mod[0])
    sh1, sc1, gt1, sh2, sc2, gt2 = [mod[:b, i * d:(i + 1) * d].reshape(b, 1, d) for i in range(6)]
    csh1 = mod[b:b + 1, 0:d]
    csc1 = mod[b:b + 1, d:2 * d]

    dec_f_k = _pair_lanes(decay_fwd[0], RET_QK_DIM)
    dec_b_k = _pair_lanes(decay_bwd[0], RET_QK_DIM)
    dec_f_v = _pair_lanes(decay_fwd[0], RET_V_DIM)
    dec_b_v = _pair_lanes(decay_bwd[0], RET_V_DIM)

    st_f, st_b = _ctx_states(ctx, csh1, csc1, w_in[0], dec_f_k, dec_b_k)

    cos_t, sin_t = _rope_tables(n)
    w_ch = jnp.asarray(_channel_dft_table(), dtype=F32).astype(BF16)
    qk, vg, fcs, wo, wg, wu, wd = _in_proj(
        x, sh1, sc1, w_in[0], w_ch, jnp.asarray(cos_t), jnp.asarray(sin_t),
        (w_out[0], w_ffn_gate[0], w_ffn_up[0], w_ffn_down[0]))

    dft_mat, tw_cos, tw_sin = (jnp.asarray(t, dtype=F32) for t in _position_dft_tables(n))
    fm = _position_dft(dft_mat, tw_cos, tw_sin, fcs)

    yn = _retention(qk, vg, st_f, st_b, dec_f_k, dec_b_k, dec_f_v, dec_b_v)

    return _out_ffn(x, fm, yn, gt1, sh2, sc2, gt2,
                    ln1_g[0].reshape(1, d), ln1_b[0].reshape(1, d), ln2_g[0].reshape(1, d), ln2_b[0].reshape(1, d),
                    wo, wg, wu, wd)
```

```python
import functools

import numpy as np
import jax
import jax.numpy as jnp
from jax import lax
from jax.experimental import pallas as pl
from jax.experimental.pallas import tpu as pltpu

F32 = jnp.float32
BF16 = jnp.bfloat16

GRID_W = 64
FOURIER_GROUPS = 4
FOURIER_GROUP_DIM = 64
FOURIER_WIDTH = FOURIER_GROUPS * FOURIER_GROUP_DIM
RET_HEADS = 6
RET_QK_DIM = 64
RET_V_DIM = 128
RET_QK_WIDTH = RET_HEADS * RET_QK_DIM
RET_WIDTH = RET_HEADS * RET_V_DIM
RET_CHUNK = 128
HEAD_PAIRS = RET_HEADS // 2
PAIR_QK = 2 * RET_QK_DIM
PAIR_V = 2 * RET_V_DIM
ROPE_BASE = 10000.0
LN_EPS = 1e-6
DEPTH = 1
DEEPNORM_ALPHA = (2.0 * DEPTH) ** 0.25

COL_F = 0
COL_Q = COL_F + FOURIER_WIDTH
COL_K = COL_Q + RET_QK_WIDTH
COL_V = COL_K + RET_QK_WIDTH
COL_G = COL_V + RET_WIDTH
PROJ_WIDTH = COL_G + RET_WIDTH

V7X_LANES = 128
V7X_VMEM_LIMIT_BYTES = 56 * 1024 * 1024

TOKEN_TILE = 512
FFN_CHUNK = 256
DFT_RADIX = 4
CTX_BATCH_GROUP = 4
WEIGHT_CAST_STEPS = 8
BF16_SUBLANES = 16


def _dot(a, b):
    return jnp.dot(a, b, preferred_element_type=F32)


def _dot_tt(a, b, lhs_dim, rhs_dim):
    return lax.dot_general(a, b, (((lhs_dim,), (rhs_dim,)), ((), ())), preferred_element_type=F32)


def _layer_norm(x):
    mu = jnp.mean(x, axis=-1, keepdims=True)
    xc = x - mu
    var = jnp.mean(xc * xc, axis=-1, keepdims=True)
    return xc * lax.rsqrt(var + LN_EPS)


def _silu(x):
    return x * jax.nn.sigmoid(x)


def _log_sigmoid(x):
    return jnp.minimum(x, 0.0) - jnp.log1p(jnp.exp(-jnp.abs(x)))


@functools.lru_cache(maxsize=None)
def _rope_tables(n_tokens):
    nf = RET_QK_DIM // 4
    t = np.arange(n_tokens)
    row = (t // GRID_W).astype(np.float32)
    col = (t % GRID_W).astype(np.float32)
    freqs = (np.float32(ROPE_BASE) ** (-np.arange(nf, dtype=np.float32) / np.float32(nf))).astype(np.float32)
    ang_r = (row[:, None] * freqs[None, :]).astype(np.float32)
    ang_c = (col[:, None] * freqs[None, :]).astype(np.float32)
    cos_h = np.concatenate([np.cos(ang_r), np.cos(ang_r), np.cos(ang_c), np.cos(ang_c)], axis=1)
    sin_h = np.concatenate([-np.sin(ang_r), np.sin(ang_r), -np.sin(ang_c), np.sin(ang_c)], axis=1)
    reps = V7X_LANES // RET_QK_DIM
    return np.concatenate([np.tile(cos_h, (1, reps)), np.tile(sin_h, (1, reps))], axis=1).astype(np.float32)


@functools.lru_cache(maxsize=None)
def _channel_dft_table():
    m = np.arange(FOURIER_GROUP_DIM)
    ang = 2.0 * np.pi * ((m[:, None] * m[None, :]) % FOURIER_GROUP_DIM) / FOURIER_GROUP_DIM
    scale = FOURIER_GROUP_DIM ** -0.5
    eye = np.eye(FOURIER_GROUPS)
    return np.concatenate([np.kron(eye, np.cos(ang) * scale), np.kron(eye, np.sin(ang) * scale)], axis=1)


@functools.lru_cache(maxsize=None)
def _position_dft_tables(n_tokens):
    sub = n_tokens // DFT_RADIX
    m = np.arange(sub, dtype=np.int64)
    ang = 2.0 * np.pi * ((m[:, None] * m[None, :]) % sub) / sub
    scale = n_tokens ** -0.5
    mat = np.concatenate([np.cos(ang) * scale, -np.sin(ang) * scale], axis=1)
    k1 = np.arange(1, DFT_RADIX, dtype=np.int64)
    phi = 2.0 * np.pi * ((k1[:, None] * m[None, :]) % n_tokens) / n_tokens
    lanes = np.ones((1, 1, V7X_LANES))
    return mat, np.cos(phi)[:, :, None] * lanes, np.sin(phi)[:, :, None] * lanes


def _modulation_kernel(c_ref, w_ref, b_ref, o_ref):
    s = _silu(c_ref[...]).astype(BF16)
    o_ref[...] = _dot(s, w_ref[...].astype(BF16)) + b_ref[...]


def _modulation(c_rows, w_mod, b_mod):
    rows, d = c_rows.shape
    width = w_mod.shape[1]
    tn = width // 4
    return pl.pallas_call(
        _modulation_kernel,
        grid=(width // tn,),
        in_specs=[
            pl.BlockSpec((rows, d), lambda j: (0, 0)),
            pl.BlockSpec((d, tn), lambda j: (0, j)),
            pl.BlockSpec((1, tn), lambda j: (0, j)),
        ],
        out_specs=pl.BlockSpec((rows, tn), lambda j: (0, j)),
        out_shape=jax.ShapeDtypeStruct((rows, width), F32),
        compiler_params=pltpu.CompilerParams(vmem_limit_bytes=V7X_VMEM_LIMIT_BYTES),
        name="modulation",
    )(c_rows, w_mod, b_mod.reshape(1, width))


def _block_diag_mask(shape):
    r = lax.broadcasted_iota(jnp.int32, shape, 0)
    c = lax.broadcasted_iota(jnp.int32, shape, 1)
    return (r >= RET_QK_DIM) == (c >= RET_V_DIM)


def _ctx_states_kernel(ctx_ref, sh_ref, sc_ref, w_ref, dfk_ref, dbk_ref, stf_ref, stb_ref):
    group, length, d = ctx_ref.shape
    x = ctx_ref[...].reshape(group * length, d)
    u = (_layer_norm(x) * (1.0 + sc_ref[...]) + sh_ref[...]).astype(BF16)
    kv = _dot(u, w_ref[:, COL_K:COL_G].astype(BF16))
    k = kv[:, :RET_QK_WIDTH] * (RET_QK_DIM ** -0.5)
    v = kv[:, RET_QK_WIDTH:].astype(BF16)
    t = lax.broadcasted_iota(jnp.int32, (length, PAIR_QK), 0).astype(F32)
    bd = _block_diag_mask((PAIR_QK, PAIR_V))
    for p in range(HEAD_PAIRS):
        w_f = jnp.exp(_log_sigmoid(dfk_ref[p]) * (length - 1.0 - t))
        w_b = jnp.exp(_log_sigmoid(dbk_ref[p]) * t)
        for bi in range(group):
            rows = slice(bi * length, (bi + 1) * length)
            k2 = k[rows, p * PAIR_QK:(p + 1) * PAIR_QK]
            v2 = v[rows, p * PAIR_V:(p + 1) * PAIR_V]
            st_f = _dot_tt((k2 * w_f).astype(BF16), v2, 0, 0)
            st_b = _dot_tt((k2 * w_b).astype(BF16), v2, 0, 0)
            stf_ref[bi, p] = jnp.where(bd, st_f, 0.0)
            stb_ref[bi, p] = jnp.where(bd, st_b, 0.0)


def _ctx_states(ctx, csh, csc, w_in, dec_f_k, dec_b_k):
    b, length, d = ctx.shape
    group = CTX_BATCH_GROUP if b % CTX_BATCH_GROUP == 0 else 1
    st_shape = jax.ShapeDtypeStruct((b, HEAD_PAIRS, PAIR_QK, PAIR_V), F32)
    st_spec = pl.BlockSpec((group, HEAD_PAIRS, PAIR_QK, PAIR_V), lambda i: (i, 0, 0, 0))
    dec_spec = pl.BlockSpec((HEAD_PAIRS, 1, PAIR_QK), lambda i: (0, 0, 0))
    return pl.pallas_call(
        _ctx_states_kernel,
        grid=(b // group,),
        in_specs=[
            pl.BlockSpec((group, length, d), lambda i: (i, 0, 0)),
            pl.BlockSpec((1, d), lambda i: (0, 0)),
            pl.BlockSpec((1, d), lambda i: (0, 0)),
            pl.BlockSpec((d, PROJ_WIDTH), lambda i: (0, 0), pipeline_mode=pl.Buffered(1)),
            dec_spec,
            dec_spec,
        ],
        out_specs=[st_spec, st_spec],
        out_shape=[st_shape, st_shape],
        compiler_params=pltpu.CompilerParams(vmem_limit_bytes=V7X_VMEM_LIMIT_BYTES),
        name="ctx_states",
    )(ctx, csh, csc, w_in, dec_f_k, dec_b_k)


def _rope(x, cos, sin_signed, first_of_pair):
    nf = RET_QK_DIM // 4
    outs = []
    for j in range(x.shape[1] // V7X_LANES):
        xj = x[:, j * V7X_LANES:(j + 1) * V7X_LANES]
        partner = jnp.where(first_of_pair, pltpu.roll(xj, V7X_LANES - nf, 1), pltpu.roll(xj, nf, 1))
        outs.append(xj * cos + partner * sin_signed)
    return jnp.concatenate(outs, axis=1)


def _tile_specs(tile_of_step, tiles_per_batch, tm, d, **kw):
    bi = lambda s: tile_of_step(s) // tiles_per_batch
    ti = lambda s: tile_of_step(s) % tiles_per_batch
    tok = lambda width: pl.BlockSpec((1, tm, width), lambda s: (bi(s), ti(s), 0), **kw)
    vec = pl.BlockSpec((1, 1, d), lambda s: (bi(s), 0, 0), **kw)
    four = pl.BlockSpec((tm, FOURIER_WIDTH), lambda s: (ti(s), bi(s)), **kw)
    rope = pl.BlockSpec((tm, 2 * V7X_LANES), lambda s: (ti(s), 0), **kw)
    return tok, vec, four, rope


def _in_proj_kernel(x0_ref, sh0_ref, sc0_ref, xn_ref, shn_ref, scn_ref, w_ref, wch_ref, rope_ref,
                    wo_ref, wg_ref, wu_ref, wd_ref,
                    qk_ref, vg_ref, fcs_ref, wo_out, wg_out, wu_out, wd_out, u_a, u_b, w_s, *, cast_steps):
    step = pl.program_id(0)

    @pl.when(step < cast_steps)
    def _():
        for src, dst in ((wo_ref, wo_out), (wg_ref, wg_out), (wu_ref, wu_out), (wd_ref, wd_out)):
            dst[...] = src[...].astype(BF16)

    def prepare(x_ref, sh_ref, sc_ref, u_dst):
        u_dst[...] = (_layer_norm(x_ref[0]) * (1.0 + sc_ref[0]) + sh_ref[0]).astype(BF16)

    def project(u_src):
        p = _dot(u_src[...], w_s[...])
        f = p[:, COL_F:COL_Q].astype(BF16)
        fcs_ref[0] = _dot(f, wch_ref[...]).astype(BF16)

        cos = rope_ref[:, :V7X_LANES]
        sin_signed = rope_ref[:, V7X_LANES:]
        lane = lax.broadcasted_iota(jnp.int32, cos.shape, 1)
        first_of_pair = jnp.bitwise_and(lane, 2 * (RET_QK_DIM // 4) - 1) < (RET_QK_DIM // 4)
        qk_ref[0, :, :RET_QK_WIDTH] = _rope(p[:, COL_Q:COL_K], cos, sin_signed, first_of_pair).astype(BF16)
        k = p[:, COL_K:COL_V] * (RET_QK_DIM ** -0.5)
        qk_ref[0, :, RET_QK_WIDTH:] = _rope(k, cos, sin_signed, first_of_pair).astype(BF16)
        vg_ref[0, :, :RET_WIDTH] = p[:, COL_V:COL_G].astype(BF16)
        vg_ref[0, :, RET_WIDTH:] = _silu(p[:, COL_G:PROJ_WIDTH]).astype(BF16)

    @pl.when(step == 0)
    def _():
        w_s[...] = w_ref[...].astype(BF16)
        prepare(x0_ref, sh0_ref, sc0_ref, u_a)

    even = lax.rem(step, 2) == 0

    @pl.when(even)
    def _():
        prepare(xn_ref, shn_ref, scn_ref, u_b)
        project(u_a)

    @pl.when(jnp.logical_not(even))
    def _():
        prepare(xn_ref, shn_ref, scn_ref, u_a)
        project(u_b)


def _cast_block_rows(rows, n_steps):
    for r in range(BF16_SUBLANES, rows + 1, BF16_SUBLANES):
        if rows % r == 0 and rows // r <= n_steps:
            return r
    raise ValueError("no aligned row blocking for the weight cast")


def _in_proj(x, sh1, sc1, w_in, w_ch, rope_t, ffn_weights):
    b, n, d = x.shape
    tm = min(TOKEN_TILE, n)
    tiles_per_batch = n // tm
    n_tiles = b * tiles_per_batch
    cast_steps = min(WEIGHT_CAST_STEPS, n_tiles)
    tok0, vec0, _, _ = _tile_specs(lambda s: 0 * s, tiles_per_batch, tm, d, pipeline_mode=pl.Buffered(1))
    tokn, vecn, _, _ = _tile_specs(lambda s: jnp.minimum(s + 1, n_tiles - 1), tiles_per_batch, tm, d)
    tok, _, _, rope = _tile_specs(lambda s: s, tiles_per_batch, tm, d)
    resident = lambda shape: pl.BlockSpec(shape, lambda s: (0, 0), pipeline_mode=pl.Buffered(1))

    cast_specs = []
    for w in ffn_weights:
        r = _cast_block_rows(w.shape[0], cast_steps)
        last = w.shape[0] // r - 1
        cast_specs.append(pl.BlockSpec((r, w.shape[1]), lambda s, last=last: (jnp.minimum(s, last), 0)))

    widths = (2 * RET_QK_WIDTH, 2 * RET_WIDTH, 2 * FOURIER_WIDTH)
    return pl.pallas_call(
        functools.partial(_in_proj_kernel, cast_steps=cast_steps),
        grid=(n_tiles,),
        in_specs=[
            tok0(d), vec0, vec0, tokn(d), vecn, vecn,
            resident((d, PROJ_WIDTH)), resident((FOURIER_WIDTH, 2 * FOURIER_WIDTH)),
            rope, *cast_specs,
        ],
        out_specs=[tok(width) for width in widths] + cast_specs,
        out_shape=[jax.ShapeDtypeStruct((b, n, width), BF16) for width in widths]
        + [jax.ShapeDtypeStruct(w.shape, BF16) for w in ffn_weights],
        scratch_shapes=[pltpu.VMEM((tm, d), BF16), pltpu.VMEM((tm, d), BF16), pltpu.VMEM((d, PROJ_WIDTH), BF16)],
        compiler_params=pltpu.CompilerParams(
            dimension_semantics=("arbitrary",), vmem_limit_bytes=V7X_VMEM_LIMIT_BYTES),
        name="in_proj",
    )(x, sh1, sc1, x, sh1, sc1, w_in, w_ch, rope_t, *ffn_weights)


def _position_dft_kernel(fcs_ref, mat_ref, twc_ref, tws_ref, o_ref):
    sub = mat_ref.shape[0]
    width = fcs_ref.shape[2] // 2
    c0, c1, c2, c3 = (fcs_ref[0, j * sub:(j + 1) * sub, :width].astype(F32) for j in range(DFT_RADIX))
    s0, s1, s2, s3 = (fcs_ref[0, j * sub:(j + 1) * sub, width:].astype(F32) for j in range(DFT_RADIX))
    tc0, tc1, tc2, tc3 = c0 + c2, c1 + c3, c0 - c2, c1 - c3
    ts0, ts1, ts2, ts3 = s0 + s2, s1 + s3, s0 - s2, s1 - s3
    parts = [
        (tc0 + tc1, ts0 + ts1),
        (tc2 - ts3, ts2 + tc3),
        (tc0 - tc1, ts0 - ts1),
        (tc2 + ts3, ts2 - tc3),
    ]
    mat = mat_ref[...].astype(BF16)
    reps = width // V7X_LANES
    for k1, (a, b) in enumerate(parts):
        if k1 > 0:
            cphi = jnp.concatenate([twc_ref[k1 - 1]] * reps, axis=1)
            sphi = jnp.concatenate([tws_ref[k1 - 1]] * reps, axis=1)
            a, b = a * cphi - b * sphi, a * sphi + b * cphi
        rhs = jnp.concatenate([a.astype(BF16), b.astype(BF16)], axis=0)
        y = _dot(mat, rhs)
        for h in range(reps):
            o_ref[0, h, pl.ds(k1, sub, stride=DFT_RADIX), :] = y[:, h * V7X_LANES:(h + 1) * V7X_LANES]


def _position_dft(mat, tw_cos, tw_sin, fcs):
    batch, n, width2 = fcs.shape
    width = width2 // 2
    resident = lambda shape: pl.BlockSpec(shape, lambda bi: (0,) * len(shape), pipeline_mode=pl.Buffered(1))
    return pl.pallas_call(
        _position_dft_kernel,
        grid=(batch,),
        in_specs=[pl.BlockSpec((1, n, width2), lambda bi: (bi, 0, 0)),
                  resident(mat.shape), resident(tw_cos.shape), resident(tw_sin.shape)],
        out_specs=pl.BlockSpec((1, width // V7X_LANES, n, V7X_LANES), lambda bi: (bi, 0, 0, 0)),
        out_shape=jax.ShapeDtypeStruct((batch, width // V7X_LANES, n, V7X_LANES), F32),
        compiler_params=pltpu.CompilerParams(vmem_limit_bytes=V7X_VMEM_LIMIT_BYTES),
        name="position_dft",
    )(fcs, mat, tw_cos, tw_sin)


def _retention_kernel(q_ref, k_ref, v_ref, g_ref, stf0_ref, stb0_ref, dfk_ref, dbk_ref, dfv_ref, dbv_ref,
                      o_ref, u_all, s_all, tab_s, xi_s):
    n = q_ref.shape[1]
    n_chunks = n // RET_CHUNK
    c = RET_CHUNK

    lgf_k = _log_sigmoid(dfk_ref[0])
    lgb_k = _log_sigmoid(dbk_ref[0])
    lgf_v = _log_sigmoid(dfv_ref[0])
    lgb_v = _log_sigmoid(dbv_ref[0])

    i_k = lax.broadcasted_iota(jnp.int32, (c, PAIR_QK), 0).astype(F32)
    i_v = lax.broadcasted_iota(jnp.int32, (c, PAIR_V), 0)
    j_v = jnp.bitwise_and(lax.broadcasted_iota(jnp.int32, (c, PAIR_V), 1), RET_V_DIM - 1)
    diff = (i_v - j_v).astype(F32)
    bd = _block_diag_mask((PAIR_QK, PAIR_V))

    tab_s[...] = jnp.where(diff >= 0.0, jnp.exp(lgf_v * jnp.maximum(diff, 0.0)),
                           jnp.exp(lgb_v * jnp.maximum(-diff, 0.0)))
    xi_s[0] = jnp.exp(lgf_k * (i_k + 1.0))
    xi_s[1] = jnp.exp(lgb_k * (c - i_k))
    g_f = jnp.exp(lgf_v * float(c))
    g_b = jnp.exp(lgb_v * float(c))
    zeta_f = jnp.exp(lgf_k * (c - 1.0 - i_k))
    zeta_b = jnp.exp(lgb_k * i_k)

    def chunk(ref, i):
        return ref[0, i * c:(i + 1) * c, :]

    for i in range(n_chunks):
        k2 = chunk(k_ref, i).astype(F32)
        kz = jnp.concatenate([(k2 * zeta_f).astype(BF16), (k2 * zeta_b).astype(BF16)], axis=1)
        u_all[i] = _dot_tt(kz, chunk(v_ref, i), 0, 0)

    s_f = stf0_ref[0, 0]
    s_b = stb0_ref[0, 0]
    for i in range(n_chunks):
        j = n_chunks - 1 - i
        s_all[i, :PAIR_QK, :] = jnp.where(bd, s_f, 0.0).astype(BF16)
        s_all[j, PAIR_QK:, :] = jnp.where(bd, s_b, 0.0).astype(BF16)
        if i + 1 < n_chunks:
            s_f = g_f * s_f + u_all[i, :PAIR_QK, :]
            s_b = g_b * s_b + u_all[j, PAIR_QK:, :]

    lane_k = lax.broadcasted_iota(jnp.int32, (c, PAIR_QK), 1)
    lane_v = lax.broadcasted_iota(jnp.int32, (c, PAIR_V), 1)
    zero_k = jnp.zeros((c, PAIR_QK), BF16)
    zero_v = jnp.zeros((c, PAIR_V), BF16)

    for i in range(n_chunks):
        q2 = chunk(q_ref, i)
        k2 = chunk(k_ref, i)
        v2 = chunk(v_ref, i)
        k_bd = jnp.concatenate([jnp.where(lane_k < RET_QK_DIM, k2, zero_k),
                                jnp.where(lane_k >= RET_QK_DIM, k2, zero_k)], axis=0)
        v_bd = jnp.concatenate([jnp.where(lane_v < RET_V_DIM, v2, zero_v),
                                jnp.where(lane_v >= RET_V_DIM, v2, zero_v)], axis=0)
        scores = _dot_tt(q2, k_bd, 1, 1)
        p = (scores * tab_s[...]).astype(BF16)
        q2f = q2.astype(F32)
        qx = jnp.concatenate([(q2f * xi_s[0]).astype(BF16), (q2f * xi_s[1]).astype(BF16)], axis=1)
        y = _dot(p, v_bd) + _dot(qx, s_all[i])
        g = chunk(g_ref, i).astype(F32)
        outs = []
        for h in range(2):
            yh = y[:, h * RET_V_DIM:(h + 1) * RET_V_DIM]
            ms = jnp.mean(yh * yh, axis=-1, keepdims=True)
            outs.append(yh * lax.rsqrt(ms + LN_EPS) * g[:, h * RET_V_DIM:(h + 1) * RET_V_DIM])
        o_ref[0, i * c:(i + 1) * c, :] = jnp.concatenate(outs, axis=1).astype(BF16)


def _retention(qk, vg, st_f, st_b, dec_f_k, dec_b_k, dec_f_v, dec_b_v):
    b, n, _ = qk.shape
    n_chunks = n // RET_CHUNK
    slab = lambda width, first: pl.BlockSpec((1, n, width), lambda bi, p: (bi, 0, first + p))
    st = pl.BlockSpec((1, 1, PAIR_QK, PAIR_V), lambda bi, p: (bi, p, 0, 0))
    dk = pl.BlockSpec((1, 1, PAIR_QK), lambda bi, p: (p, 0, 0))
    dv = pl.BlockSpec((1, 1, PAIR_V), lambda bi, p: (p, 0, 0))
    return pl.pallas_call(
        _retention_kernel,
        grid=(b, HEAD_PAIRS),
        in_specs=[slab(PAIR_QK, 0), slab(PAIR_QK, HEAD_PAIRS), slab(PAIR_V, 0), slab(PAIR_V, HEAD_PAIRS),
                  st, st, dk, dk, dv, dv],
        out_specs=slab(PAIR_V, 0),
        out_shape=jax.ShapeDtypeStruct((b, n, RET_WIDTH), BF16),
        scratch_shapes=[
            pltpu.VMEM((n_chunks, 2 * PAIR_QK, PAIR_V), F32),
            pltpu.VMEM((n_chunks, 2 * PAIR_QK, PAIR_V), BF16),
            pltpu.VMEM((RET_CHUNK, PAIR_V), F32),
            pltpu.VMEM((2, RET_CHUNK, PAIR_QK), F32),
        ],
        compiler_params=pltpu.CompilerParams(vmem_limit_bytes=V7X_VMEM_LIMIT_BYTES),
        name="retention",
    )(qk, qk, vg, vg, st_f, st_b, dec_f_k, dec_b_k, dec_f_v, dec_b_v)


def _out_ffn_kernel(x0_ref, fm0_ref, yn0_ref, gt10_ref, sh20_ref, sc20_ref,
                    xn_ref, fmn_ref, ynn_ref, gt1n_ref, sh2n_ref, sc2n_ref,
                    gt2_ref, l1g_ref, l1b_ref, l2g_ref, l2b_ref, wo_ref, wg_ref, wu_ref, wd_ref,
                    o_ref, x1_a, x1_b, u2_a, u2_b, z_a, z_b):
    step = pl.program_id(0)
    last = pl.num_programs(0) - 1

    def prepare(x_ref, fm_ref, yn_ref, gt1_ref, sh2_ref, sc2_ref, x1_dst, u2_dst):
        fm = jnp.concatenate([fm_ref[0, h] for h in range(fm_ref.shape[1])], axis=1).astype(BF16)
        mix = _dot(fm, wo_ref[:FOURIER_WIDTH, :]) + _dot(yn_ref[0], wo_ref[FOURIER_WIDTH:, :])
        x1 = _layer_norm(DEEPNORM_ALPHA * x_ref[0] + gt1_ref[0] * mix) * l1g_ref[...] + l1b_ref[...]
        x1_dst[...] = x1
        u2_dst[...] = (_layer_norm(x1) * (1.0 + sc2_ref[0]) + sh2_ref[0]).astype(BF16)

    def ffn(x1_src, u2_src, z_dst):
        d_ff = wg_ref.shape[1]
        acc = None
        for c0 in range(0, d_ff, FFN_CHUNK):
            c1 = c0 + FFN_CHUNK
            u2 = u2_src[...]
            h = (_silu(_dot(u2, wg_ref[:, c0:c1])) * _dot(u2, wu_ref[:, c0:c1])).astype(BF16)
            part = _dot(h, wd_ref[c0:c1, :])
            acc = part if acc is None else acc + part
        z_dst[...] = DEEPNORM_ALPHA * x1_src[...] + gt2_ref[0] * acc

    def finalize(z_src):
        o_ref[0] = _layer_norm(z_src[...]) * l2g_ref[...] + l2b_ref[...]

    @pl.when(step == 0)
    def _():
        prepare(x0_ref, fm0_ref, yn0_ref, gt10_ref, sh20_ref, sc20_ref, x1_a, u2_a)
        z_b[...] = jnp.zeros(z_b.shape, z_b.dtype)

    even = lax.rem(step, 2) == 0
    running = step < last

    @pl.when(jnp.logical_and(running, even))
    def _():
        prepare(xn_ref, fmn_ref, ynn_ref, gt1n_ref, sh2n_ref, sc2n_ref, x1_b, u2_b)
        ffn(x1_a, u2_a, z_a)
        finalize(z_b)

    @pl.when(jnp.logical_and(running, jnp.logical_not(even)))
    def _():
        prepare(xn_ref, fmn_ref, ynn_ref, gt1n_ref, sh2n_ref, sc2n_ref, x1_a, u2_a)
        ffn(x1_b, u2_b, z_b)
        finalize(z_a)

    @pl.when(jnp.logical_and(jnp.logical_not(running), even))
    def _():
        finalize(z_b)

    @pl.when(jnp.logical_and(jnp.logical_not(running), jnp.logical_not(even)))
    def _():
        finalize(z_a)


def _out_ffn(x, fm, yn, gt1, sh2, sc2, gt2, l1g, l1b, l2g, l2b, wo, wg, wu, wd):
    b, n, d = x.shape
    d_ff = wg.shape[1]
    assert d_ff % FFN_CHUNK == 0
    tm = min(TOKEN_TILE, n)
    tiles_per_batch = n // tm
    n_tiles = b * tiles_per_batch
    planes = fm.shape[1]

    def specs(tile_of_step, **kw):
        tok, vec, _, _ = _tile_specs(tile_of_step, tiles_per_batch, tm, d, **kw)
        bi = lambda s: tile_of_step(s) // tiles_per_batch
        ti = lambda s: tile_of_step(s) % tiles_per_batch
        four = pl.BlockSpec((1, planes, tm, V7X_LANES), lambda s: (bi(s), 0, ti(s), 0), **kw)
        return tok, vec, four

    tok0, vec0, four0 = specs(lambda s: 0 * s, pipeline_mode=pl.Buffered(1))
    tokn, vecn, fourn = specs(lambda s: jnp.minimum(s + 1, n_tiles - 1))
    _, vecc, _ = specs(lambda s: jnp.minimum(s, n_tiles - 1))
    tokp, _, _ = specs(lambda s: jnp.maximum(s - 1, 0))
    par = pl.BlockSpec((1, d), lambda s: (0, 0))
    resident = lambda shape: pl.BlockSpec(shape, lambda s: (0, 0), pipeline_mode=pl.Buffered(1))
    return pl.pallas_call(
        _out_ffn_kernel,
        grid=(n_tiles + 1,),
        in_specs=[
            tok0(d), four0, tok0(RET_WIDTH), vec0, vec0, vec0,
            tokn(d), fourn, tokn(RET_WIDTH), vecn, vecn, vecn,
            vecc, par, par, par, par,
            resident((d, d)), resident((d, d_ff)), resident((d, d_ff)), resident((d_ff, d)),
        ],
        out_specs=tokp(d),
        out_shape=jax.ShapeDtypeStruct((b, n, d), x.dtype),
        scratch_shapes=[
            pltpu.VMEM((tm, d), F32), pltpu.VMEM((tm, d), F32),
            pltpu.VMEM((tm, d), BF16), pltpu.VMEM((tm, d), BF16),
            pltpu.VMEM((tm, d), F32), pltpu.VMEM((tm, d), F32),
        ],
        compiler_params=pltpu.CompilerParams(
            dimension_semantics=("arbitrary",), vmem_limit_bytes=V7X_VMEM_LIMIT_BYTES),
        name="out_ffn",
    )(x, fm, yn, gt1, sh2, sc2, x, fm, yn, gt1, sh2, sc2, gt2, l1g, l1b, l2g, l2b, wo, wg, wu, wd)


def _pair_lanes(decay, width):
    return jnp.repeat(decay.astype(F32), width).reshape(HEAD_PAIRS, 1, 2 * width)


def kernel(x, c, ctx, c_ctx, w_mod, b_mod, w_in, w_out, decay_fwd, decay_bwd, ln1_g, ln1_b,
           w_ffn_gate, w_ffn_up, w_ffn_down, ln2_g, ln2_b):
    b, n, d = x.shape
    assert w_mod.shape[0] == DEPTH and n % RET_CHUNK == 0 and n % GRID_W == 0 and n % (8 * DFT_RADIX) == 0
    assert ctx.shape[1] % RET_CHUNK == 0 and w_in.shape[2] == PROJ_WIDTH

    rows = -(-(b + 1) // 8) * 8
    c_rows = jnp.zeros((rows, d), F32).at[:b].set(c).at[b].set(c_ctx)
    mod = _modulation(c_rows, w_mod[0], b_mod[0])
    sh1, sc1, gt1, sh2, sc2, gt2 = [mod[:b, i * d:(i + 1) * d].reshape(b, 1, d) for i in range(6)]
    csh1 = mod[b:b + 1, 0:d]
    csc1 = mod[b:b + 1, d:2 * d]

    dec_f_k = _pair_lanes(decay_fwd[0], RET_QK_DIM)
    dec_b_k = _pair_lanes(decay_bwd[0], RET_QK_DIM)
    dec_f_v = _pair_lanes(decay_fwd[0], RET_V_DIM)
    dec_b_v = _pair_lanes(decay_bwd[0], RET_V_DIM)

    st_f, st_b = _ctx_states(ctx, csh1, csc1, w_in[0], dec_f_k, dec_b_k)

    w_ch = jnp.asarray(_channel_dft_table(), dtype=F32).astype(BF16)
    qk, vg, fcs, wo, wg, wu, wd = _in_proj(
        x, sh1, sc1, w_in[0], w_ch, jnp.asarray(_rope_tables(n)),
        (w_out[0], w_ffn_gate[0], w_ffn_up[0], w_ffn_down[0]))

    dft_mat, tw_cos, tw_sin = (jnp.asarray(t, dtype=F32) for t in _position_dft_tables(n))
    fm = _position_dft(dft_mat, tw_cos, tw_sin, fcs)

    yn = _retention(qk, vg, st_f, st_b, dec_f_k, dec_b_k, dec_f_v, dec_b_v)

    return _out_ffn(x, fm, yn, gt1, sh2, sc2, gt2,
                    ln1_g[0].reshape(1, d), ln1_b[0].reshape(1, d), ln2_g[0].reshape(1, d), ln2_b[0].reshape(1, d),
                    wo, wg, wu, wd)
```

```python
import functools

import numpy as np
import jax
import jax.numpy as jnp
from jax import lax
from jax.experimental import pallas as pl
from jax.experimental.pallas import tpu as pltpu

F32 = jnp.float32
BF16 = jnp.bfloat16

GRID_W = 64
FOURIER_GROUPS = 4
FOURIER_GROUP_DIM = 64
FOURIER_WIDTH = FOURIER_GROUPS * FOURIER_GROUP_DIM
RET_HEADS = 6
RET_QK_DIM = 64
RET_V_DIM = 128
RET_QK_WIDTH = RET_HEADS * RET_QK_DIM
RET_WIDTH = RET_HEADS * RET_V_DIM
RET_CHUNK = 128
HEAD_PAIRS = RET_HEADS // 2
PAIR_QK = 2 * RET_QK_DIM
PAIR_V = 2 * RET_V_DIM
ROPE_BASE = 10000.0
LN_EPS = 1e-6
DEPTH = 1
DEEPNORM_ALPHA = (2.0 * DEPTH) ** 0.25

COL_F = 0
COL_Q = COL_F + FOURIER_WIDTH
COL_K = COL_Q + RET_QK_WIDTH
COL_V = COL_K + RET_QK_WIDTH
COL_G = COL_V + RET_WIDTH
PROJ_WIDTH = COL_G + RET_WIDTH

V7X_LANES = 128
V7X_VMEM_LIMIT_BYTES = 56 * 1024 * 1024

TOKEN_TILE = 512
IN_PROJ_TILE = 1024
DFT_BATCH_GROUP = 2
FFN_CHUNK = 256
DFT_RADIX = 4
CTX_BATCH_GROUP = 4
BF16_SUBLANES = 16


def _dot(a, b):
    return jnp.dot(a, b, preferred_element_type=F32)


def _dot_tt(a, b, lhs_dim, rhs_dim):
    return lax.dot_general(a, b, (((lhs_dim,), (rhs_dim,)), ((), ())), preferred_element_type=F32)


def _layer_norm(x):
    mu = jnp.mean(x, axis=-1, keepdims=True)
    xc = x - mu
    var = jnp.mean(xc * xc, axis=-1, keepdims=True)
    return xc * lax.rsqrt(var + LN_EPS)


def _silu(x):
    return x * jax.nn.sigmoid(x)


def _log_sigmoid(x):
    return jnp.minimum(x, 0.0) - jnp.log1p(jnp.exp(-jnp.abs(x)))


@functools.lru_cache(maxsize=None)
def _rope_tables(n_tokens):
    nf = RET_QK_DIM // 4
    t = np.arange(n_tokens)
    row = (t // GRID_W).astype(np.float32)
    col = (t % GRID_W).astype(np.float32)
    freqs = (np.float32(ROPE_BASE) ** (-np.arange(nf, dtype=np.float32) / np.float32(nf))).astype(np.float32)
    ang_r = (row[:, None] * freqs[None, :]).astype(np.float32)
    ang_c = (col[:, None] * freqs[None, :]).astype(np.float32)
    cos_h = np.concatenate([np.cos(ang_r), np.cos(ang_r), np.cos(ang_c), np.cos(ang_c)], axis=1)
    sin_h = np.concatenate([-np.sin(ang_r), np.sin(ang_r), -np.sin(ang_c), np.sin(ang_c)], axis=1)
    reps = V7X_LANES // RET_QK_DIM
    return (np.tile(cos_h, (1, reps)).astype(np.float32), np.tile(sin_h, (1, reps)).astype(np.float32))


@functools.lru_cache(maxsize=None)
def _channel_dft_table():
    m = np.arange(FOURIER_GROUP_DIM)
    ang = 2.0 * np.pi * ((m[:, None] * m[None, :]) % FOURIER_GROUP_DIM) / FOURIER_GROUP_DIM
    scale = FOURIER_GROUP_DIM ** -0.5
    eye = np.eye(FOURIER_GROUPS)
    return np.concatenate([np.kron(eye, np.cos(ang) * scale), np.kron(eye, np.sin(ang) * scale)], axis=1)


@functools.lru_cache(maxsize=None)
def _position_dft_tables(n_tokens):
    sub = n_tokens // DFT_RADIX
    m = np.arange(sub, dtype=np.int64)
    ang = 2.0 * np.pi * ((m[:, None] * m[None, :]) % sub) / sub
    scale = n_tokens ** -0.5
    mat = np.concatenate([np.cos(ang) * scale, -np.sin(ang) * scale], axis=1)
    k1 = np.arange(1, DFT_RADIX, dtype=np.int64)
    phi = 2.0 * np.pi * ((k1[:, None] * m[None, :]) % n_tokens) / n_tokens
    lanes = np.ones((1, 1, V7X_LANES))
    return mat, np.cos(phi)[:, :, None] * lanes, np.sin(phi)[:, :, None] * lanes


def _modulation_kernel(c_ref, w_ref, b_ref, o_ref):
    s = _silu(c_ref[...]).astype(BF16)
    o_ref[...] = _dot(s, w_ref[...].astype(BF16)) + b_ref[...]


def _modulation(c_rows, w_mod, b_mod):
    rows, d = c_rows.shape
    width = w_mod.shape[1]
    tn = width // 4
    return pl.pallas_call(
        _modulation_kernel,
        grid=(width // tn,),
        in_specs=[
            pl.BlockSpec((rows, d), lambda j: (0, 0)),
            pl.BlockSpec((d, tn), lambda j: (0, j)),
            pl.BlockSpec((1, tn), lambda j: (0, j)),
        ],
        out_specs=pl.BlockSpec((rows, tn), lambda j: (0, j)),
        out_shape=jax.ShapeDtypeStruct((rows, width), F32),
        compiler_params=pltpu.CompilerParams(vmem_limit_bytes=V7X_VMEM_LIMIT_BYTES),
        name="modulation",
    )(c_rows, w_mod, b_mod.reshape(1, width))


def _block_diag_mask(shape):
    r = lax.broadcasted_iota(jnp.int32, shape, 0)
    c = lax.broadcasted_iota(jnp.int32, shape, 1)
    return (r >= RET_QK_DIM) == (c >= RET_V_DIM)


def _ctx_states_kernel(ctx_ref, sh_ref, sc_ref, w_ref, dfk_ref, dbk_ref, stf_ref, stb_ref, wq_ref):
    group, length, d = ctx_ref.shape

    @pl.when(pl.program_id(0) == 0)
    def _():
        wq_ref[...] = w_ref[...].astype(BF16)

    x = ctx_ref[...].reshape(group * length, d)
    u = (_layer_norm(x) * (1.0 + sc_ref[...]) + sh_ref[...]).astype(BF16)
    kv = _dot(u, w_ref[:, COL_K:COL_G].astype(BF16))
    k = kv[:, :RET_QK_WIDTH] * (RET_QK_DIM ** -0.5)
    v = kv[:, RET_QK_WIDTH:].astype(BF16)
    t = lax.broadcasted_iota(jnp.int32, (length, PAIR_QK), 0).astype(F32)
    bd = _block_diag_mask((PAIR_QK, PAIR_V))
    for p in range(HEAD_PAIRS):
        w_f = jnp.exp(_log_sigmoid(dfk_ref[p]) * (length - 1.0 - t))
        w_b = jnp.exp(_log_sigmoid(dbk_ref[p]) * t)
        for bi in range(group):
            rows = slice(bi * length, (bi + 1) * length)
            k2 = k[rows, p * PAIR_QK:(p + 1) * PAIR_QK]
            v2 = v[rows, p * PAIR_V:(p + 1) * PAIR_V]
            st_f = _dot_tt((k2 * w_f).astype(BF16), v2, 0, 0)
            st_b = _dot_tt((k2 * w_b).astype(BF16), v2, 0, 0)
            stf_ref[bi, p] = jnp.where(bd, st_f, 0.0)
            stb_ref[bi, p] = jnp.where(bd, st_b, 0.0)


def _ctx_states(ctx, csh, csc, w_in, dec_f_k, dec_b_k):
    b, length, d = ctx.shape
    group = CTX_BATCH_GROUP if b % CTX_BATCH_GROUP == 0 else 1
    st_shape = jax.ShapeDtypeStruct((b, HEAD_PAIRS, PAIR_QK, PAIR_V), F32)
    st_spec = pl.BlockSpec((group, HEAD_PAIRS, PAIR_QK, PAIR_V), lambda i: (i, 0, 0, 0))
    dec_spec = pl.BlockSpec((HEAD_PAIRS, 1, PAIR_QK), lambda i: (0, 0, 0))
    return pl.pallas_call(
        _ctx_states_kernel,
        grid=(b // group,),
        in_specs=[
            pl.BlockSpec((group, length, d), lambda i: (i, 0, 0)),
            pl.BlockSpec((1, d), lambda i: (0, 0)),
            pl.BlockSpec((1, d), lambda i: (0, 0)),
            pl.BlockSpec((d, PROJ_WIDTH), lambda i: (0, 0), pipeline_mode=pl.Buffered(1)),
            dec_spec,
            dec_spec,
        ],
        out_specs=[st_spec, st_spec, pl.BlockSpec((d, PROJ_WIDTH), lambda i: (0, 0))],
        out_shape=[st_shape, st_shape, jax.ShapeDtypeStruct((d, PROJ_WIDTH), BF16)],
        compiler_params=pltpu.CompilerParams(
            dimension_semantics=("arbitrary",), vmem_limit_bytes=V7X_VMEM_LIMIT_BYTES),
        name="ctx_states",
    )(ctx, csh, csc, w_in, dec_f_k, dec_b_k)


def _rope(x, cos, sin_signed, first_of_pair):
    nf = RET_QK_DIM // 4
    outs = []
    for j in range(x.shape[1] // V7X_LANES):
        xj = x[:, j * V7X_LANES:(j + 1) * V7X_LANES]
        partner = jnp.where(first_of_pair, pltpu.roll(xj, V7X_LANES - nf, 1), pltpu.roll(xj, nf, 1))
        outs.append(xj * cos + partner * sin_signed)
    return jnp.concatenate(outs, axis=1)


def _tile_specs(tile_of_step, tiles_per_batch, tm, d, **kw):
    bi = lambda s: tile_of_step(s) // tiles_per_batch
    ti = lambda s: tile_of_step(s) % tiles_per_batch
    tok = lambda width: pl.BlockSpec((1, tm, width), lambda s: (bi(s), ti(s), 0), **kw)
    vec = pl.BlockSpec((1, 1, d), lambda s: (bi(s), 0, 0), **kw)
    four = pl.BlockSpec((tm, FOURIER_WIDTH), lambda s: (ti(s), bi(s)), **kw)
    rope = pl.BlockSpec((tm, V7X_LANES), lambda s: (ti(s), 0), **kw)
    return tok, vec, four, rope


def _in_proj_kernel(x0_ref, sh0_ref, sc0_ref, xn_ref, shn_ref, scn_ref, w_ref, wch_ref, cos_ref, sin_ref,
                    wo_ref, wg_ref, wu_ref, wd_ref,
                    qk_ref, vg_ref, fcs_ref, wo_out, wg_out, wu_out, wd_out, u_a, u_b):
    step = pl.program_id(0)
    for src, dst in ((wo_ref, wo_out), (wg_ref, wg_out), (wu_ref, wu_out), (wd_ref, wd_out)):
        dst[...] = src[...].astype(BF16)

    def prepare(x_ref, sh_ref, sc_ref, u_dst):
        u_dst[...] = (_layer_norm(x_ref[0]) * (1.0 + sc_ref[0]) + sh_ref[0]).astype(BF16)

    def project(u_src):
        p = _dot(u_src[...], w_ref[...])
        f = p[:, COL_F:COL_Q].astype(BF16)
        fcs_ref[0] = _dot(f, wch_ref[...]).astype(BF16)

        cos = cos_ref[...]
        sin_signed = sin_ref[...]
        lane = lax.broadcasted_iota(jnp.int32, cos.shape, 1)
        first_of_pair = jnp.bitwise_and(lane, 2 * (RET_QK_DIM // 4) - 1) < (RET_QK_DIM // 4)
        qk_ref[0, :, :RET_QK_WIDTH] = _rope(p[:, COL_Q:COL_K], cos, sin_signed, first_of_pair).astype(BF16)
        k = p[:, COL_K:COL_V] * (RET_QK_DIM ** -0.5)
        qk_ref[0, :, RET_QK_WIDTH:] = _rope(k, cos, sin_signed, first_of_pair).astype(BF16)
        vg_ref[0, :, :RET_WIDTH] = p[:, COL_V:COL_G].astype(BF16)
        vg_ref[0, :, RET_WIDTH:] = _silu(p[:, COL_G:PROJ_WIDTH]).astype(BF16)

    @pl.when(step == 0)
    def _():
        prepare(x0_ref, sh0_ref, sc0_ref, u_a)

    even = lax.rem(step, 2) == 0

    @pl.when(even)
    def _():
        prepare(xn_ref, shn_ref, scn_ref, u_b)
        project(u_a)

    @pl.when(jnp.logical_not(even))
    def _():
        prepare(xn_ref, shn_ref, scn_ref, u_a)
        project(u_b)


def _cast_block_rows(rows, n_steps):
    for r in range(BF16_SUBLANES, rows + 1, BF16_SUBLANES):
        if rows % r == 0 and rows // r <= n_steps:
            return r
    raise ValueError("no aligned row blocking for the weight cast")


def _in_proj(x, sh1, sc1, w_in_bf16, w_ch, cos_t, sin_t, ffn_weights):
    b, n, d = x.shape
    tm = min(IN_PROJ_TILE, n)
    tiles_per_batch = n // tm
    n_tiles = b * tiles_per_batch
    tok0, vec0, _, _ = _tile_specs(lambda s: 0 * s, tiles_per_batch, tm, d, pipeline_mode=pl.Buffered(1))
    tokn, vecn, _, _ = _tile_specs(lambda s: jnp.minimum(s + 1, n_tiles - 1), tiles_per_batch, tm, d)
    tok, _, _, rope = _tile_specs(lambda s: s, tiles_per_batch, tm, d)
    resident = lambda shape: pl.BlockSpec(shape, lambda s: (0, 0), pipeline_mode=pl.Buffered(1))

    cast_specs = []
    for w in ffn_weights:
        r = _cast_block_rows(w.shape[0], n_tiles)
        last = w.shape[0] // r - 1
        cast_specs.append(pl.BlockSpec((r, w.shape[1]), lambda s, last=last: (jnp.minimum(s, last), 0)))

    widths = (2 * RET_QK_WIDTH, 2 * RET_WIDTH, 2 * FOURIER_WIDTH)
    return pl.pallas_call(
        _in_proj_kernel,
        grid=(n_tiles,),
        in_specs=[
            tok0(d), vec0, vec0, tokn(d), vecn, vecn,
            resident((d, PROJ_WIDTH)), resident((FOURIER_WIDTH, 2 * FOURIER_WIDTH)),
            rope, rope, *cast_specs,
        ],
        out_specs=[tok(width) for width in widths] + cast_specs,
        out_shape=[jax.ShapeDtypeStruct((b, n, width), BF16) for width in widths]
        + [jax.ShapeDtypeStruct(w.shape, BF16) for w in ffn_weights],
        scratch_shapes=[pltpu.VMEM((tm, d), BF16), pltpu.VMEM((tm, d), BF16)],
        compiler_params=pltpu.CompilerParams(
            dimension_semantics=("arbitrary",), vmem_limit_bytes=V7X_VMEM_LIMIT_BYTES),
        name="in_proj",
    )(x, sh1, sc1, x, sh1, sc1, w_in_bf16, w_ch, cos_t, sin_t, *ffn_weights)


def _position_dft_kernel(fcs_ref, mat_ref, twc_ref, tws_ref, o_ref):
    sub = mat_ref.shape[0]
    width = fcs_ref.shape[2] // 2
    reps = width // V7X_LANES
    mat = mat_ref[...].astype(BF16)
    for bi in range(fcs_ref.shape[0]):
        c0, c1, c2, c3 = (fcs_ref[bi, j * sub:(j + 1) * sub, :width].astype(F32) for j in range(DFT_RADIX))
        s0, s1, s2, s3 = (fcs_ref[bi, j * sub:(j + 1) * sub, width:].astype(F32) for j in range(DFT_RADIX))
        tc0, tc1, tc2, tc3 = c0 + c2, c1 + c3, c0 - c2, c1 - c3
        ts0, ts1, ts2, ts3 = s0 + s2, s1 + s3, s0 - s2, s1 - s3
        parts = [
            (tc0 + tc1, ts0 + ts1),
            (tc2 - ts3, ts2 + tc3),
            (tc0 - tc1, ts0 - ts1),
            (tc2 + ts3, ts2 - tc3),
        ]
        for k1, (a, b) in enumerate(parts):
            if k1 > 0:
                cphi = jnp.concatenate([twc_ref[k1 - 1]] * reps, axis=1)
                sphi = jnp.concatenate([tws_ref[k1 - 1]] * reps, axis=1)
                a, b = a * cphi - b * sphi, a * sphi + b * cphi
            rhs = jnp.concatenate([a.astype(BF16), b.astype(BF16)], axis=0)
            y = _dot(mat, rhs)
            for h in range(reps):
                o_ref[bi, h, pl.ds(k1, sub, stride=DFT_RADIX), :] = y[:, h * V7X_LANES:(h + 1) * V7X_LANES]


def _position_dft(mat, tw_cos, tw_sin, fcs):
    batch, n, width2 = fcs.shape
    width = width2 // 2
    group = DFT_BATCH_GROUP if batch % DFT_BATCH_GROUP == 0 else 1
    resident = lambda shape: pl.BlockSpec(shape, lambda bi: (0,) * len(shape), pipeline_mode=pl.Buffered(1))
    return pl.pallas_call(
        _position_dft_kernel,
        grid=(batch // group,),
        in_specs=[pl.BlockSpec((group, n, width2), lambda bi: (bi, 0, 0)),
                  resident(mat.shape), resident(tw_cos.shape), resident(tw_sin.shape)],
        out_specs=pl.BlockSpec((group, width // V7X_LANES, n, V7X_LANES), lambda bi: (bi, 0, 0, 0)),
        out_shape=jax.ShapeDtypeStruct((batch, width // V7X_LANES, n, V7X_LANES), F32),
        compiler_params=pltpu.CompilerParams(vmem_limit_bytes=V7X_VMEM_LIMIT_BYTES),
        name="position_dft",
    )(fcs, mat, tw_cos, tw_sin)


def _retention_kernel(qk_ref, vg_ref, stf0_ref, stb0_ref, dfk_ref, dbk_ref, dfv_ref, dbv_ref,
                      o_ref, u_all, s_all, tab_s, xi_s):
    n = qk_ref.shape[1]
    n_chunks = n // RET_CHUNK
    c = RET_CHUNK

    i_k = lax.broadcasted_iota(jnp.int32, (c, PAIR_QK), 0).astype(F32)
    i_v = lax.broadcasted_iota(jnp.int32, (c, PAIR_V), 0)
    j_v = jnp.bitwise_and(lax.broadcasted_iota(jnp.int32, (c, PAIR_V), 1), RET_V_DIM - 1)
    diff = (i_v - j_v).astype(F32)
    bd = _block_diag_mask((PAIR_QK, PAIR_V))
    lane_k = lax.broadcasted_iota(jnp.int32, (c, PAIR_QK), 1)
    lane_v = lax.broadcasted_iota(jnp.int32, (c, PAIR_V), 1)
    zero_k = jnp.zeros((c, PAIR_QK), BF16)
    zero_v = jnp.zeros((c, PAIR_V), BF16)

    for p in range(HEAD_PAIRS):
        rows = lambda i: slice(i * c, (i + 1) * c)
        q_chunk = lambda i: qk_ref[0, rows(i), p * PAIR_QK:(p + 1) * PAIR_QK]
        k_chunk = lambda i: qk_ref[0, rows(i), RET_QK_WIDTH + p * PAIR_QK:RET_QK_WIDTH + (p + 1) * PAIR_QK]
        v_chunk = lambda i: vg_ref[0, rows(i), p * PAIR_V:(p + 1) * PAIR_V]
        g_chunk = lambda i: vg_ref[0, rows(i), RET_WIDTH + p * PAIR_V:RET_WIDTH + (p + 1) * PAIR_V]

        lgf_k = _log_sigmoid(dfk_ref[p])
        lgb_k = _log_sigmoid(dbk_ref[p])
        lgf_v = _log_sigmoid(dfv_ref[p])
        lgb_v = _log_sigmoid(dbv_ref[p])

        tab_s[p] = jnp.where(diff >= 0.0, jnp.exp(lgf_v * jnp.maximum(diff, 0.0)),
                             jnp.exp(lgb_v * jnp.maximum(-diff, 0.0)))
        xi_s[p, 0] = jnp.exp(lgf_k * (i_k + 1.0))
        xi_s[p, 1] = jnp.exp(lgb_k * (c - i_k))
        g_f = jnp.exp(lgf_v * float(c))
        g_b = jnp.exp(lgb_v * float(c))
        zeta_f = jnp.exp(lgf_k * (c - 1.0 - i_k))
        zeta_b = jnp.exp(lgb_k * i_k)

        for i in range(n_chunks):
            k2 = k_chunk(i).astype(F32)
            kz = jnp.concatenate([(k2 * zeta_f).astype(BF16), (k2 * zeta_b).astype(BF16)], axis=1)
            u_all[p, i] = _dot_tt(kz, v_chunk(i), 0, 0)

        s_f = stf0_ref[0, p]
        s_b = stb0_ref[0, p]
        for i in range(n_chunks):
            j = n_chunks - 1 - i
            s_all[p, i, :PAIR_QK, :] = jnp.where(bd, s_f, 0.0).astype(BF16)
            s_all[p, j, PAIR_QK:, :] = jnp.where(bd, s_b, 0.0).astype(BF16)
            if i + 1 < n_chunks:
                s_f = g_f * s_f + u_all[p, i, :PAIR_QK, :]
                s_b = g_b * s_b + u_all[p, j, PAIR_QK:, :]

        for i in range(n_chunks):
            q2 = q_chunk(i)
            k2 = k_chunk(i)
            v2 = v_chunk(i)
            k_bd = jnp.concatenate([jnp.where(lane_k < RET_QK_DIM, k2, zero_k),
                                    jnp.where(lane_k >= RET_QK_DIM, k2, zero_k)], axis=0)
            v_bd = jnp.concatenate([jnp.where(lane_v < RET_V_DIM, v2, zero_v),
                                    jnp.where(lane_v >= RET_V_DIM, v2, zero_v)], axis=0)
            scores = _dot_tt(q2, k_bd, 1, 1)
            pr = (scores * tab_s[p]).astype(BF16)
            q2f = q2.astype(F32)
            qx = jnp.concatenate([(q2f * xi_s[p, 0]).astype(BF16), (q2f * xi_s[p, 1]).astype(BF16)], axis=1)
            y = _dot(pr, v_bd) + _dot(qx, s_all[p, i])
            g = g_chunk(i).astype(F32)
            outs = []
            for h in range(2):
                yh = y[:, h * RET_V_DIM:(h + 1) * RET_V_DIM]
                ms = jnp.mean(yh * yh, axis=-1, keepdims=True)
                outs.append(yh * lax.rsqrt(ms + LN_EPS) * g[:, h * RET_V_DIM:(h + 1) * RET_V_DIM])
            o_ref[0, rows(i), p * PAIR_V:(p + 1) * PAIR_V] = jnp.concatenate(outs, axis=1).astype(BF16)


def _retention(qk, vg, st_f, st_b, dec_f_k, dec_b_k, dec_f_v, dec_b_v):
    b, n, _ = qk.shape
    n_chunks = n // RET_CHUNK
    seq = lambda width: pl.BlockSpec((1, n, width), lambda bi: (bi, 0, 0))
    st = pl.BlockSpec((1, HEAD_PAIRS, PAIR_QK, PAIR_V), lambda bi: (bi, 0, 0, 0))
    dk = pl.BlockSpec((HEAD_PAIRS, 1, PAIR_QK), lambda bi: (0, 0, 0))
    dv = pl.BlockSpec((HEAD_PAIRS, 1, PAIR_V), lambda bi: (0, 0, 0))
    return pl.pallas_call(
        _retention_kernel,
        grid=(b,),
        in_specs=[seq(2 * RET_QK_WIDTH), seq(2 * RET_WIDTH), st, st, dk, dk, dv, dv],
        out_specs=seq(RET_WIDTH),
        out_shape=jax.ShapeDtypeStruct((b, n, RET_WIDTH), BF16),
        scratch_shapes=[
            pltpu.VMEM((HEAD_PAIRS, n_chunks, 2 * PAIR_QK, PAIR_V), F32),
            pltpu.VMEM((HEAD_PAIRS, n_chunks, 2 * PAIR_QK, PAIR_V), BF16),
            pltpu.VMEM((HEAD_PAIRS, RET_CHUNK, PAIR_V), F32),
            pltpu.VMEM((HEAD_PAIRS, 2, RET_CHUNK, PAIR_QK), F32),
        ],
        compiler_params=pltpu.CompilerParams(vmem_limit_bytes=V7X_VMEM_LIMIT_BYTES),
        name="retention",
    )(qk, vg, st_f, st_b, dec_f_k, dec_b_k, dec_f_v, dec_b_v)


def _out_ffn_kernel(x0_ref, fm0_ref, yn0_ref, gt10_ref, sh20_ref, sc20_ref,
                    xn_ref, fmn_ref, ynn_ref, gt1n_ref, sh2n_ref, sc2n_ref,
                    gt2_ref, l1g_ref, l1b_ref, l2g_ref, l2b_ref, wo_ref, wg_ref, wu_ref, wd_ref,
                    o_ref, x1_a, x1_b, u2_a, u2_b):
    step = pl.program_id(0)

    def prepare(x_ref, fm_ref, yn_ref, gt1_ref, sh2_ref, sc2_ref, x1_dst, u2_dst):
        fm = jnp.concatenate([fm_ref[0, h] for h in range(fm_ref.shape[1])], axis=1).astype(BF16)
        mix = _dot(fm, wo_ref[:FOURIER_WIDTH, :]) + _dot(yn_ref[0], wo_ref[FOURIER_WIDTH:, :])
        x1 = _layer_norm(DEEPNORM_ALPHA * x_ref[0] + gt1_ref[0] * mix) * l1g_ref[...] + l1b_ref[...]
        x1_dst[...] = x1
        u2_dst[...] = (_layer_norm(x1) * (1.0 + sc2_ref[0]) + sh2_ref[0]).astype(BF16)

    def ffn(x1_src, u2_src):
        d_ff = wg_ref.shape[1]
        acc = None
        for c0 in range(0, d_ff, FFN_CHUNK):
            c1 = c0 + FFN_CHUNK
            u2 = u2_src[...]
            h = (_silu(_dot(u2, wg_ref[:, c0:c1])) * _dot(u2, wu_ref[:, c0:c1])).astype(BF16)
            part = _dot(h, wd_ref[c0:c1, :])
            acc = part if acc is None else acc + part
        z = DEEPNORM_ALPHA * x1_src[...] + gt2_ref[0] * acc
        o_ref[0] = _layer_norm(z) * l2g_ref[...] + l2b_ref[...]

    @pl.when(step == 0)
    def _():
        prepare(x0_ref, fm0_ref, yn0_ref, gt10_ref, sh20_ref, sc20_ref, x1_a, u2_a)

    even = lax.rem(step, 2) == 0

    @pl.when(even)
    def _():
        prepare(xn_ref, fmn_ref, ynn_ref, gt1n_ref, sh2n_ref, sc2n_ref, x1_b, u2_b)
        ffn(x1_a, u2_a)

    @pl.when(jnp.logical_not(even))
    def _():
        prepare(xn_ref, fmn_ref, ynn_ref, gt1n_ref, sh2n_ref, sc2n_ref, x1_a, u2_a)
        ffn(x1_b, u2_b)


def _out_ffn(x, fm, yn, gt1, sh2, sc2, gt2, l1g, l1b, l2g, l2b, wo, wg, wu, wd):
    b, n, d = x.shape
    d_ff = wg.shape[1]
    assert d_ff % FFN_CHUNK == 0
    tm = min(TOKEN_TILE, n)
    tiles_per_batch = n // tm
    n_tiles = b * tiles_per_batch
    planes = fm.shape[1]

    def specs(tile_of_step, **kw):
        tok, vec, _, _ = _tile_specs(tile_of_step, tiles_per_batch, tm, d, **kw)
        bi = lambda s: tile_of_step(s) // tiles_per_batch
        ti = lambda s: tile_of_step(s) % tiles_per_batch
        four = pl.BlockSpec((1, planes, tm, V7X_LANES), lambda s: (bi(s), 0, ti(s), 0), **kw)
        return tok, vec, four

    tok0, vec0, four0 = specs(lambda s: 0 * s, pipeline_mode=pl.Buffered(1))
    tokn, vecn, fourn = specs(lambda s: jnp.minimum(s + 1, n_tiles - 1))
    tokc, vecc, _ = specs(lambda s: s)
    par = pl.BlockSpec((1, d), lambda s: (0, 0))
    resident = lambda shape: pl.BlockSpec(shape, lambda s: (0, 0), pipeline_mode=pl.Buffered(1))
    return pl.pallas_call(
        _out_ffn_kernel,
        grid=(n_tiles,),
        in_specs=[
            tok0(d), four0, tok0(RET_WIDTH), vec0, vec0, vec0,
            tokn(d), fourn, tokn(RET_WIDTH), vecn, vecn, vecn,
            vecc, par, par, par, par,
            resident((d, d)), resident((d, d_ff)), resident((d, d_ff)), resident((d_ff, d)),
        ],
        out_specs=tokc(d),
        out_shape=jax.ShapeDtypeStruct((b, n, d), x.dtype),
        scratch_shapes=[
            pltpu.VMEM((tm, d), F32), pltpu.VMEM((tm, d), F32),
            pltpu.VMEM((tm, d), BF16), pltpu.VMEM((tm, d), BF16),
        ],
        compiler_params=pltpu.CompilerParams(
            dimension_semantics=("arbitrary",), vmem_limit_bytes=V7X_VMEM_LIMIT_BYTES),
        name="out_ffn",
    )(x, fm, yn, gt1, sh2, sc2, x, fm, yn, gt1, sh2, sc2, gt2, l1g, l1b, l2g, l2b, wo, wg, wu, wd)


def _pair_lanes(decay, width):
    return jnp.repeat(decay.astype(F32), width).reshape(HEAD_PAIRS, 1, 2 * width)


def kernel(x, c, ctx, c_ctx, w_mod, b_mod, w_in, w_out, decay_fwd, decay_bwd, ln1_g, ln1_b,
           w_ffn_gate, w_ffn_up, w_ffn_down, ln2_g, ln2_b):
    b, n, d = x.shape
    assert w_mod.shape[0] == DEPTH and n % RET_CHUNK == 0 and n % GRID_W == 0 and n % (8 * DFT_RADIX) == 0
    assert ctx.shape[1] % RET_CHUNK == 0 and w_in.shape[2] == PROJ_WIDTH

    rows = -(-(b + 1) // 8) * 8
    c_rows = jnp.zeros((rows, d), F32).at[:b].set(c).at[b].set(c_ctx)
    mod = _modulation(c_rows, w_mod[0], b_mod[0])
    sh1, sc1, gt1, sh2, sc2, gt2 = [mod[:b, i * d:(i + 1) * d].reshape(b, 1, d) for i in range(6)]
    csh1 = mod[b:b + 1, 0:d]
    csc1 = mod[b:b + 1, d:2 * d]

    dec_f_k = _pair_lanes(decay_fwd[0], RET_QK_DIM)
    dec_b_k = _pair_lanes(decay_bwd[0], RET_QK_DIM)
    dec_f_v = _pair_lanes(decay_fwd[0], RET_V_DIM)
    dec_b_v = _pair_lanes(decay_bwd[0], RET_V_DIM)

    st_f, st_b, w_in_bf16 = _ctx_states(ctx, csh1, csc1, w_in[0], dec_f_k, dec_b_k)

    cos_t, sin_t = _rope_tables(n)
    w_ch = jnp.asarray(_channel_dft_table(), dtype=F32).astype(BF16)
    qk, vg, fcs, wo, wg, wu, wd = _in_proj(
        x, sh1, sc1, w_in_bf16, w_ch, jnp.asarray(cos_t), jnp.asarray(sin_t),
        (w_out[0], w_ffn_gate[0], w_ffn_up[0], w_ffn_down[0]))

    dft_mat, tw_cos, tw_sin = (jnp.asarray(t, dtype=F32) for t in _position_dft_tables(n))
    fm = _position_dft(dft_mat, tw_cos, tw_sin, fcs)

    yn = _retention(qk, vg, st_f, st_b, dec_f_k, dec_b_k, dec_f_v, dec_b_v)

    return _out_ffn(x, fm, yn, gt1, sh2, sc2, gt2,
                    ln1_g[0].reshape(1, d), ln1_b[0].reshape(1, d), ln2_g[0].reshape(1, d), ln2_b[0].reshape(1, d),
                    wo, wg, wu, wd)
```

```python
import functools

import numpy as np
import jax
import jax.numpy as jnp
from jax import lax
from jax.experimental import pallas as pl
from jax.experimental.pallas import tpu as pltpu

F32 = jnp.float32
BF16 = jnp.bfloat16

GRID_W = 64
FOURIER_GROUPS = 4
FOURIER_GROUP_DIM = 64
FOURIER_WIDTH = FOURIER_GROUPS * FOURIER_GROUP_DIM
RET_HEADS = 6
RET_QK_DIM = 64
RET_V_DIM = 128
RET_QK_WIDTH = RET_HEADS * RET_QK_DIM
RET_WIDTH = RET_HEADS * RET_V_DIM
RET_CHUNK = 128
HEAD_PAIRS = RET_HEADS // 2
PAIR_QK = 2 * RET_QK_DIM
PAIR_V = 2 * RET_V_DIM
ROPE_BASE = 10000.0
LN_EPS = 1e-6
DEPTH = 1
DEEPNORM_ALPHA = (2.0 * DEPTH) ** 0.25

COL_F = 0
COL_Q = COL_F + FOURIER_WIDTH
COL_K = COL_Q + RET_QK_WIDTH
COL_V = COL_K + RET_QK_WIDTH
COL_G = COL_V + RET_WIDTH
PROJ_WIDTH = COL_G + RET_WIDTH

V7X_LANES = 128
V7X_SUBLANES = 8
V7X_VMEM_LIMIT_BYTES = 56 * 1024 * 1024

TOKEN_TILE = 512
IN_PROJ_TILE = 1024
DFT_BATCH_GROUP = 2
FFN_CHUNK = 256
DFT_RADIX = 4
CTX_BATCH_GROUP = 4
BF16_SUBLANES = 16


def _dot(a, b):
    return jnp.dot(a, b, preferred_element_type=F32)


def _dot_tt(a, b, lhs_dim, rhs_dim):
    return lax.dot_general(a, b, (((lhs_dim,), (rhs_dim,)), ((), ())), preferred_element_type=F32)


def _layer_norm(x):
    mu = jnp.mean(x, axis=-1, keepdims=True)
    xc = x - mu
    var = jnp.mean(xc * xc, axis=-1, keepdims=True)
    return xc * lax.rsqrt(var + LN_EPS)


def _silu(x):
    return x * jax.nn.sigmoid(x)


def _log_sigmoid(x):
    return jnp.minimum(x, 0.0) - jnp.log1p(jnp.exp(-jnp.abs(x)))


@functools.lru_cache(maxsize=None)
def _rope_tables(n_tokens):
    nf = RET_QK_DIM // 4
    t = np.arange(n_tokens)
    row = (t // GRID_W).astype(np.float32)
    col = (t % GRID_W).astype(np.float32)
    freqs = (np.float32(ROPE_BASE) ** (-np.arange(nf, dtype=np.float32) / np.float32(nf))).astype(np.float32)
    ang_r = (row[:, None] * freqs[None, :]).astype(np.float32)
    ang_c = (col[:, None] * freqs[None, :]).astype(np.float32)
    cos_h = np.concatenate([np.cos(ang_r), np.cos(ang_r), np.cos(ang_c), np.cos(ang_c)], axis=1)
    sin_h = np.concatenate([-np.sin(ang_r), np.sin(ang_r), -np.sin(ang_c), np.sin(ang_c)], axis=1)
    reps = V7X_LANES // RET_QK_DIM
    return (np.tile(cos_h, (1, reps)).astype(np.float32), np.tile(sin_h, (1, reps)).astype(np.float32))


@functools.lru_cache(maxsize=None)
def _channel_dft_table():
    m = np.arange(FOURIER_GROUP_DIM)
    ang = 2.0 * np.pi * ((m[:, None] * m[None, :]) % FOURIER_GROUP_DIM) / FOURIER_GROUP_DIM
    scale = FOURIER_GROUP_DIM ** -0.5
    eye = np.eye(FOURIER_GROUPS)
    return np.concatenate([np.kron(eye, np.cos(ang) * scale), np.kron(eye, np.sin(ang) * scale)], axis=1)


@functools.lru_cache(maxsize=None)
def _position_dft_tables(n_tokens):
    sub = n_tokens // DFT_RADIX
    m = np.arange(sub, dtype=np.int64)
    ang = 2.0 * np.pi * ((m[:, None] * m[None, :]) % sub) / sub
    scale = n_tokens ** -0.5
    mat = np.concatenate([np.cos(ang) * scale, -np.sin(ang) * scale], axis=1)
    k1 = np.arange(1, DFT_RADIX, dtype=np.int64)
    phi = 2.0 * np.pi * ((k1[:, None] * m[None, :]) % n_tokens) / n_tokens
    lanes = np.ones((1, 1, V7X_LANES))
    return mat, np.cos(phi)[:, :, None] * lanes, np.sin(phi)[:, :, None] * lanes


def _modulation_kernel(c_ref, w_ref, b_ref, o_ref):
    s = _silu(c_ref[...]).astype(BF16)
    o_ref[...] = _dot(s, w_ref[...].astype(BF16)) + b_ref[...]


def _modulation(c_rows, w_mod, b_mod):
    rows, d = c_rows.shape
    width = w_mod.shape[1]
    tn = width // 4
    return pl.pallas_call(
        _modulation_kernel,
        grid=(width // tn,),
        in_specs=[
            pl.BlockSpec((rows, d), lambda j: (0, 0)),
            pl.BlockSpec((d, tn), lambda j: (0, j)),
            pl.BlockSpec((1, tn), lambda j: (0, j)),
        ],
        out_specs=pl.BlockSpec((rows, tn), lambda j: (0, j)),
        out_shape=jax.ShapeDtypeStruct((rows, width), F32),
        compiler_params=pltpu.CompilerParams(vmem_limit_bytes=V7X_VMEM_LIMIT_BYTES),
        name="modulation",
    )(c_rows, w_mod, b_mod.reshape(1, width))


def _block_diag_mask(shape):
    r = lax.broadcasted_iota(jnp.int32, shape, 0)
    c = lax.broadcasted_iota(jnp.int32, shape, 1)
    return (r >= RET_QK_DIM) == (c >= RET_V_DIM)


def _ctx_states_kernel(ctx_ref, sh_ref, sc_ref, w_ref, dfk_ref, dbk_ref, stf_ref, stb_ref, wq_ref):
    group, length, d = ctx_ref.shape

    @pl.when(pl.program_id(0) == 0)
    def _():
        wq_ref[...] = w_ref[...].astype(BF16)

    x = ctx_ref[...].reshape(group * length, d)
    u = (_layer_norm(x) * (1.0 + sc_ref[...]) + sh_ref[...]).astype(BF16)
    kv = _dot(u, w_ref[:, COL_K:COL_G].astype(BF16))
    k = kv[:, :RET_QK_WIDTH] * (RET_QK_DIM ** -0.5)
    v = kv[:, RET_QK_WIDTH:].astype(BF16)
    t = lax.broadcasted_iota(jnp.int32, (length, PAIR_QK), 0).astype(F32)
    bd = _block_diag_mask((PAIR_QK, PAIR_V))
    for p in range(HEAD_PAIRS):
        w_f = jnp.exp(_log_sigmoid(dfk_ref[p]) * (length - 1.0 - t))
        w_b = jnp.exp(_log_sigmoid(dbk_ref[p]) * t)
        for bi in range(group):
            rows = slice(bi * length, (bi + 1) * length)
            k2 = k[rows, p * PAIR_QK:(p + 1) * PAIR_QK]
            v2 = v[rows, p * PAIR_V:(p + 1) * PAIR_V]
            st_f = _dot_tt((k2 * w_f).astype(BF16), v2, 0, 0)
            st_b = _dot_tt((k2 * w_b).astype(BF16), v2, 0, 0)
            stf_ref[bi, p] = jnp.where(bd, st_f, 0.0)
            stb_ref[bi, p] = jnp.where(bd, st_b, 0.0)


def _ctx_states(ctx, csh, csc, w_in, dec_f_k, dec_b_k):
    b, length, d = ctx.shape
    group = CTX_BATCH_GROUP if b % CTX_BATCH_GROUP == 0 else 1
    st_shape = jax.ShapeDtypeStruct((b, HEAD_PAIRS, PAIR_QK, PAIR_V), F32)
    st_spec = pl.BlockSpec((group, HEAD_PAIRS, PAIR_QK, PAIR_V), lambda i: (i, 0, 0, 0))
    dec_spec = pl.BlockSpec((HEAD_PAIRS, 1, PAIR_QK), lambda i: (0, 0, 0))
    return pl.pallas_call(
        _ctx_states_kernel,
        grid=(b // group,),
        in_specs=[
            pl.BlockSpec((group, length, d), lambda i: (i, 0, 0)),
            pl.BlockSpec((1, d), lambda i: (0, 0)),
            pl.BlockSpec((1, d), lambda i: (0, 0)),
            pl.BlockSpec((d, PROJ_WIDTH), lambda i: (0, 0), pipeline_mode=pl.Buffered(1)),
            dec_spec,
            dec_spec,
        ],
        out_specs=[st_spec, st_spec, pl.BlockSpec((d, PROJ_WIDTH), lambda i: (0, 0))],
        out_shape=[st_shape, st_shape, jax.ShapeDtypeStruct((d, PROJ_WIDTH), BF16)],
        compiler_params=pltpu.CompilerParams(
            dimension_semantics=("arbitrary",), vmem_limit_bytes=V7X_VMEM_LIMIT_BYTES),
        name="ctx_states",
    )(ctx, csh, csc, w_in, dec_f_k, dec_b_k)


def _rope(x, cos, sin_signed, first_of_pair):
    nf = RET_QK_DIM // 4
    outs = []
    for j in range(x.shape[1] // V7X_LANES):
        xj = x[:, j * V7X_LANES:(j + 1) * V7X_LANES]
        partner = jnp.where(first_of_pair, pltpu.roll(xj, V7X_LANES - nf, 1), pltpu.roll(xj, nf, 1))
        outs.append(xj * cos + partner * sin_signed)
    return jnp.concatenate(outs, axis=1)


def _tile_specs(tile_of_step, tiles_per_batch, tm, d, **kw):
    bi = lambda s: tile_of_step(s) // tiles_per_batch
    ti = lambda s: tile_of_step(s) % tiles_per_batch
    tok = lambda width: pl.BlockSpec((1, tm, width), lambda s: (bi(s), ti(s), 0), **kw)
    vec = pl.BlockSpec((1, 1, d), lambda s: (bi(s), 0, 0), **kw)
    four = pl.BlockSpec((tm, FOURIER_WIDTH), lambda s: (ti(s), bi(s)), **kw)
    rope = pl.BlockSpec((tm, V7X_LANES), lambda s: (ti(s), 0), **kw)
    return tok, vec, four, rope


def _in_proj_kernel(x0_ref, sh0_ref, sc0_ref, xn_ref, shn_ref, scn_ref, w_ref, wch_ref, cos_ref, sin_ref,
                    wo_ref, wg_ref, wu_ref, wd_ref,
                    qk_ref, vg_ref, fcs_ref, wo_out, wg_out, wu_out, wd_out, u_a, u_b):
    step = pl.program_id(0)
    for src, dst in ((wo_ref, wo_out), (wg_ref, wg_out), (wu_ref, wu_out), (wd_ref, wd_out)):
        dst[...] = src[...].astype(BF16)

    def prepare(x_ref, sh_ref, sc_ref, u_dst):
        u_dst[...] = (_layer_norm(x_ref[0]) * (1.0 + sc_ref[0]) + sh_ref[0]).astype(BF16)

    def project(u_src):
        p = _dot(u_src[...], w_ref[...])
        f = p[:, COL_F:COL_Q].astype(BF16)
        fcs_ref[0] = _dot(f, wch_ref[...]).astype(BF16)

        cos = cos_ref[...]
        sin_signed = sin_ref[...]
        lane = lax.broadcasted_iota(jnp.int32, cos.shape, 1)
        first_of_pair = jnp.bitwise_and(lane, 2 * (RET_QK_DIM // 4) - 1) < (RET_QK_DIM // 4)
        qk_ref[0, :, :RET_QK_WIDTH] = _rope(p[:, COL_Q:COL_K], cos, sin_signed, first_of_pair).astype(BF16)
        k = p[:, COL_K:COL_V] * (RET_QK_DIM ** -0.5)
        qk_ref[0, :, RET_QK_WIDTH:] = _rope(k, cos, sin_signed, first_of_pair).astype(BF16)
        vg_ref[0, :, :RET_WIDTH] = p[:, COL_V:COL_G].astype(BF16)
        vg_ref[0, :, RET_WIDTH:] = _silu(p[:, COL_G:PROJ_WIDTH]).astype(BF16)

    @pl.when(step == 0)
    def _():
        prepare(x0_ref, sh0_ref, sc0_ref, u_a)

    even = lax.rem(step, 2) == 0

    @pl.when(even)
    def _():
        prepare(xn_ref, shn_ref, scn_ref, u_b)
        project(u_a)

    @pl.when(jnp.logical_not(even))
    def _():
        prepare(xn_ref, shn_ref, scn_ref, u_a)
        project(u_b)


def _cast_block_rows(rows, n_steps):
    for r in range(BF16_SUBLANES, rows + 1, BF16_SUBLANES):
        if rows % r == 0 and rows // r <= n_steps:
            return r
    raise ValueError("no aligned row blocking for the weight cast")


def _in_proj(x, sh1, sc1, w_in_bf16, w_ch, cos_t, sin_t, ffn_weights):
    b, n, d = x.shape
    tm = min(IN_PROJ_TILE, n)
    tiles_per_batch = n // tm
    n_tiles = b * tiles_per_batch
    tok0, vec0, _, _ = _tile_specs(lambda s: 0 * s, tiles_per_batch, tm, d, pipeline_mode=pl.Buffered(1))
    tokn, vecn, _, _ = _tile_specs(lambda s: jnp.minimum(s + 1, n_tiles - 1), tiles_per_batch, tm, d)
    tok, _, _, rope = _tile_specs(lambda s: s, tiles_per_batch, tm, d)
    resident = lambda shape: pl.BlockSpec(shape, lambda s: (0, 0), pipeline_mode=pl.Buffered(1))

    cast_specs = []
    for w in ffn_weights:
        r = _cast_block_rows(w.shape[0], n_tiles)
        last = w.shape[0] // r - 1
        cast_specs.append(pl.BlockSpec((r, w.shape[1]), lambda s, last=last: (jnp.minimum(s, last), 0)))

    widths = (2 * RET_QK_WIDTH, 2 * RET_WIDTH, 2 * FOURIER_WIDTH)
    return pl.pallas_call(
        _in_proj_kernel,
        grid=(n_tiles,),
        in_specs=[
            tok0(d), vec0, vec0, tokn(d), vecn, vecn,
            resident((d, PROJ_WIDTH)), resident((FOURIER_WIDTH, 2 * FOURIER_WIDTH)),
            rope, rope, *cast_specs,
        ],
        out_specs=[tok(width) for width in widths] + cast_specs,
        out_shape=[jax.ShapeDtypeStruct((b, n, width), BF16) for width in widths]
        + [jax.ShapeDtypeStruct(w.shape, BF16) for w in ffn_weights],
        scratch_shapes=[pltpu.VMEM((tm, d), BF16), pltpu.VMEM((tm, d), BF16)],
        compiler_params=pltpu.CompilerParams(
            dimension_semantics=("arbitrary",), vmem_limit_bytes=V7X_VMEM_LIMIT_BYTES),
        name="in_proj",
    )(x, sh1, sc1, x, sh1, sc1, w_in_bf16, w_ch, cos_t, sin_t, *ffn_weights)


def _position_dft_kernel(fcs_ref, mat_ref, twc_ref, tws_ref, o_ref):
    sub = mat_ref.shape[0]
    width = fcs_ref.shape[2] // 2
    reps = width // V7X_LANES
    mat = mat_ref[...].astype(BF16)
    for bi in range(fcs_ref.shape[0]):
        c0, c1, c2, c3 = (fcs_ref[bi, j * sub:(j + 1) * sub, :width].astype(F32) for j in range(DFT_RADIX))
        s0, s1, s2, s3 = (fcs_ref[bi, j * sub:(j + 1) * sub, width:].astype(F32) for j in range(DFT_RADIX))
        tc0, tc1, tc2, tc3 = c0 + c2, c1 + c3, c0 - c2, c1 - c3
        ts0, ts1, ts2, ts3 = s0 + s2, s1 + s3, s0 - s2, s1 - s3
        parts = [
            (tc0 + tc1, ts0 + ts1),
            (tc2 - ts3, ts2 + tc3),
            (tc0 - tc1, ts0 - ts1),
            (tc2 + ts3, ts2 - tc3),
        ]
        for k1, (a, b) in enumerate(parts):
            if k1 > 0:
                cphi = jnp.concatenate([twc_ref[k1 - 1]] * reps, axis=1)
                sphi = jnp.concatenate([tws_ref[k1 - 1]] * reps, axis=1)
                a, b = a * cphi - b * sphi, a * sphi + b * cphi
            rhs = jnp.concatenate([a.astype(BF16), b.astype(BF16)], axis=0)
            y = _dot(mat, rhs)
            for h in range(reps):
                o_ref[bi, h, pl.ds(k1, sub, stride=DFT_RADIX), :] = y[:, h * V7X_LANES:(h + 1) * V7X_LANES]


def _position_dft(mat, tw_cos, tw_sin, fcs):
    batch, n, width2 = fcs.shape
    width = width2 // 2
    group = DFT_BATCH_GROUP if batch % DFT_BATCH_GROUP == 0 else 1
    resident = lambda shape: pl.BlockSpec(shape, lambda bi: (0,) * len(shape), pipeline_mode=pl.Buffered(1))
    return pl.pallas_call(
        _position_dft_kernel,
        grid=(batch // group,),
        in_specs=[pl.BlockSpec((group, n, width2), lambda bi: (bi, 0, 0)),
                  resident(mat.shape), resident(tw_cos.shape), resident(tw_sin.shape)],
        out_specs=pl.BlockSpec((group, width // V7X_LANES, n, V7X_LANES), lambda bi: (bi, 0, 0, 0)),
        out_shape=jax.ShapeDtypeStruct((batch, width // V7X_LANES, n, V7X_LANES), F32),
        compiler_params=pltpu.CompilerParams(vmem_limit_bytes=V7X_VMEM_LIMIT_BYTES),
        name="position_dft",
    )(fcs, mat, tw_cos, tw_sin)


def _retention_kernel(qk_ref, vg_ref, stf0_ref, stb0_ref, dfk_ref, dbk_ref, dfv_ref, dbv_ref,
                      o_ref, u_all, s_all, tab_s, xi_s):
    n = qk_ref.shape[1]
    n_chunks = n // RET_CHUNK
    c = RET_CHUNK

    i_k = lax.broadcasted_iota(jnp.int32, (c, PAIR_QK), 0).astype(F32)
    i_v = lax.broadcasted_iota(jnp.int32, (c, PAIR_V), 0)
    j_v = jnp.bitwise_and(lax.broadcasted_iota(jnp.int32, (c, PAIR_V), 1), RET_V_DIM - 1)
    diff = (i_v - j_v).astype(F32)
    bd = _block_diag_mask((PAIR_QK, PAIR_V))
    lane_k = lax.broadcasted_iota(jnp.int32, (c, PAIR_QK), 1)
    lane_v = lax.broadcasted_iota(jnp.int32, (c, PAIR_V), 1)
    zero_k = jnp.zeros((c, PAIR_QK), BF16)
    zero_v = jnp.zeros((c, PAIR_V), BF16)

    for p in range(HEAD_PAIRS):
        rows = lambda i: slice(i * c, (i + 1) * c)
        q_chunk = lambda i: qk_ref[0, rows(i), p * PAIR_QK:(p + 1) * PAIR_QK]
        k_chunk = lambda i: qk_ref[0, rows(i), RET_QK_WIDTH + p * PAIR_QK:RET_QK_WIDTH + (p + 1) * PAIR_QK]
        v_chunk = lambda i: vg_ref[0, rows(i), p * PAIR_V:(p + 1) * PAIR_V]
        g_chunk = lambda i: vg_ref[0, rows(i), RET_WIDTH + p * PAIR_V:RET_WIDTH + (p + 1) * PAIR_V]

        lgf_k = _log_sigmoid(dfk_ref[p])
        lgb_k = _log_sigmoid(dbk_ref[p])
        lgf_v = _log_sigmoid(dfv_ref[p])
        lgb_v = _log_sigmoid(dbv_ref[p])

        tab_s[p] = jnp.where(diff >= 0.0, jnp.exp(lgf_v * jnp.maximum(diff, 0.0)),
                             jnp.exp(lgb_v * jnp.maximum(-diff, 0.0)))
        xi_s[p, 0] = jnp.exp(lgf_k * (i_k + 1.0))
        xi_s[p, 1] = jnp.exp(lgb_k * (c - i_k))
        g_f = jnp.exp(lgf_v * float(c))
        g_b = jnp.exp(lgb_v * float(c))
        zeta_f = jnp.exp(lgf_k * (c - 1.0 - i_k))
        zeta_b = jnp.exp(lgb_k * i_k)

        for i in range(n_chunks):
            k2 = k_chunk(i).astype(F32)
            kz = jnp.concatenate([(k2 * zeta_f).astype(BF16), (k2 * zeta_b).astype(BF16)], axis=1)
            u_all[p, i] = _dot_tt(kz, v_chunk(i), 0, 0)

        s_f = stf0_ref[0, p]
        s_b = stb0_ref[0, p]
        for i in range(n_chunks):
            j = n_chunks - 1 - i
            s_all[p, i, :PAIR_QK, :] = jnp.where(bd, s_f, 0.0).astype(BF16)
            s_all[p, j, PAIR_QK:, :] = jnp.where(bd, s_b, 0.0).astype(BF16)
            if i + 1 < n_chunks:
                s_f = g_f * s_f + u_all[p, i, :PAIR_QK, :]
                s_b = g_b * s_b + u_all[p, j, PAIR_QK:, :]

        for i in range(n_chunks):
            q2 = q_chunk(i)
            k2 = k_chunk(i)
            v2 = v_chunk(i)
            k_bd = jnp.concatenate([jnp.where(lane_k < RET_QK_DIM, k2, zero_k),
                                    jnp.where(lane_k >= RET_QK_DIM, k2, zero_k)], axis=0)
            v_bd = jnp.concatenate([jnp.where(lane_v < RET_V_DIM, v2, zero_v),
                                    jnp.where(lane_v >= RET_V_DIM, v2, zero_v)], axis=0)
            scores = _dot_tt(q2, k_bd, 1, 1)
            pr = (scores * tab_s[p]).astype(BF16)
            q2f = q2.astype(F32)
            qx = jnp.concatenate([(q2f * xi_s[p, 0]).astype(BF16), (q2f * xi_s[p, 1]).astype(BF16)], axis=1)
            y = _dot(pr, v_bd) + _dot(qx, s_all[p, i])
            g = g_chunk(i).astype(F32)
            outs = []
            for h in range(2):
                yh = y[:, h * RET_V_DIM:(h + 1) * RET_V_DIM]
                ms = jnp.mean(yh * yh, axis=-1, keepdims=True)
                outs.append(yh * lax.rsqrt(ms + LN_EPS) * g[:, h * RET_V_DIM:(h + 1) * RET_V_DIM])
            o_ref[0, rows(i), p * PAIR_V:(p + 1) * PAIR_V] = jnp.concatenate(outs, axis=1).astype(BF16)


def _retention(qk, vg, st_f, st_b, dec_f_k, dec_b_k, dec_f_v, dec_b_v):
    b, n, _ = qk.shape
    n_chunks = n // RET_CHUNK
    seq = lambda width: pl.BlockSpec((1, n, width), lambda bi: (bi, 0, 0))
    st = pl.BlockSpec((1, HEAD_PAIRS, PAIR_QK, PAIR_V), lambda bi: (bi, 0, 0, 0))
    dk = pl.BlockSpec((HEAD_PAIRS, 1, PAIR_QK), lambda bi: (0, 0, 0))
    dv = pl.BlockSpec((HEAD_PAIRS, 1, PAIR_V), lambda bi: (0, 0, 0))
    return pl.pallas_call(
        _retention_kernel,
        grid=(b,),
        in_specs=[seq(2 * RET_QK_WIDTH), seq(2 * RET_WIDTH), st, st, dk, dk, dv, dv],
        out_specs=seq(RET_WIDTH),
        out_shape=jax.ShapeDtypeStruct((b, n, RET_WIDTH), BF16),
        scratch_shapes=[
            pltpu.VMEM((HEAD_PAIRS, n_chunks, 2 * PAIR_QK, PAIR_V), F32),
            pltpu.VMEM((HEAD_PAIRS, n_chunks, 2 * PAIR_QK, PAIR_V), BF16),
            pltpu.VMEM((HEAD_PAIRS, RET_CHUNK, PAIR_V), F32),
            pltpu.VMEM((HEAD_PAIRS, 2, RET_CHUNK, PAIR_QK), F32),
        ],
        compiler_params=pltpu.CompilerParams(vmem_limit_bytes=V7X_VMEM_LIMIT_BYTES),
        name="retention",
    )(qk, vg, st_f, st_b, dec_f_k, dec_b_k, dec_f_v, dec_b_v)


def _out_ffn_kernel(x0_ref, fm0_ref, yn0_ref, gt10_ref, sh20_ref, sc20_ref,
                    xb_ref, fmb_ref, ynb_ref, gt1b_ref, sh2b_ref, sc2b_ref,
                    xa_ref, fma_ref, yna_ref, gt1a_ref, sh2a_ref, sc2a_ref,
                    gt2a_ref, gt2b_ref, l1g_ref, l1b_ref, l2g_ref, l2b_ref, wo_ref, wg_ref, wu_ref, wd_ref,
                    o_ref, x1_a, x1_b, u2_a, u2_b):
    tm = x1_a.shape[0]

    def prepare(x_ref, fm_ref, yn_ref, gt1_ref, sh2_ref, sc2_ref, x1_dst, u2_dst):
        fm = jnp.concatenate([fm_ref[0, h] for h in range(fm_ref.shape[1])], axis=1).astype(BF16)
        mix = _dot(fm, wo_ref[:FOURIER_WIDTH, :]) + _dot(yn_ref[0], wo_ref[FOURIER_WIDTH:, :])
        x1 = _layer_norm(DEEPNORM_ALPHA * x_ref[0] + gt1_ref[0] * mix) * l1g_ref[...] + l1b_ref[...]
        x1_dst[...] = x1
        u2_dst[...] = (_layer_norm(x1) * (1.0 + sc2_ref[0]) + sh2_ref[0]).astype(BF16)

    def ffn(x1_src, u2_src, gt2_ref, row0):
        d_ff = wg_ref.shape[1]
        acc = None
        for c0 in range(0, d_ff, FFN_CHUNK):
            c1 = c0 + FFN_CHUNK
            u2 = u2_src[...]
            h = (_silu(_dot(u2, wg_ref[:, c0:c1])) * _dot(u2, wu_ref[:, c0:c1])).astype(BF16)
            part = _dot(h, wd_ref[c0:c1, :])
            acc = part if acc is None else acc + part
        z = DEEPNORM_ALPHA * x1_src[...] + gt2_ref[0] * acc
        o_ref[0, row0:row0 + tm, :] = _layer_norm(z) * l2g_ref[...] + l2b_ref[...]

    @pl.when(pl.program_id(0) == 0)
    def _():
        prepare(x0_ref, fm0_ref, yn0_ref, gt10_ref, sh20_ref, sc20_ref, x1_a, u2_a)

    prepare(xb_ref, fmb_ref, ynb_ref, gt1b_ref, sh2b_ref, sc2b_ref, x1_b, u2_b)
    ffn(x1_a, u2_a, gt2a_ref, 0)
    prepare(xa_ref, fma_ref, yna_ref, gt1a_ref, sh2a_ref, sc2a_ref, x1_a, u2_a)
    ffn(x1_b, u2_b, gt2b_ref, tm)


def _out_ffn(x, fm, yn, gt1, sh2, sc2, gt2, l1g, l1b, l2g, l2b, wo, wg, wu, wd):
    b, n, d = x.shape
    d_ff = wg.shape[1]
    assert d_ff % FFN_CHUNK == 0
    tm = min(TOKEN_TILE, n // 2)
    tiles_per_batch = n // tm
    n_tiles = b * tiles_per_batch
    assert tiles_per_batch % 2 == 0
    planes = fm.shape[1]

    def specs(tile_of_step, **kw):
        tok, vec, _, _ = _tile_specs(tile_of_step, tiles_per_batch, tm, d, **kw)
        bi = lambda s: tile_of_step(s) // tiles_per_batch
        ti = lambda s: tile_of_step(s) % tiles_per_batch
        four = pl.BlockSpec((1, planes, tm, V7X_LANES), lambda s: (bi(s), 0, ti(s), 0), **kw)
        return tok, vec, four

    tok0, vec0, four0 = specs(lambda g: 0 * g, pipeline_mode=pl.Buffered(1))
    tokb, vecb, fourb = specs(lambda g: 2 * g + 1)
    toka, veca, foura = specs(lambda g: jnp.minimum(2 * g + 2, n_tiles - 1))
    _, vecc, _ = specs(lambda g: 2 * g)
    par = pl.BlockSpec((1, d), lambda g: (0, 0))
    resident = lambda shape: pl.BlockSpec(shape, lambda g: (0, 0), pipeline_mode=pl.Buffered(1))
    pairs_per_batch = tiles_per_batch // 2
    out_spec = pl.BlockSpec((1, 2 * tm, d), lambda g: (g // pairs_per_batch, g % pairs_per_batch, 0))
    return pl.pallas_call(
        _out_ffn_kernel,
        grid=(n_tiles // 2,),
        in_specs=[
            tok0(d), four0, tok0(RET_WIDTH), vec0, vec0, vec0,
            tokb(d), fourb, tokb(RET_WIDTH), vecb, vecb, vecb,
            toka(d), foura, toka(RET_WIDTH), veca, veca, veca,
            vecc, vecb, par, par, par, par,
            resident((d, d)), resident((d, d_ff)), resident((d, d_ff)), resident((d_ff, d)),
        ],
        out_specs=out_spec,
        out_shape=jax.ShapeDtypeStruct((b, n, d), x.dtype),
        scratch_shapes=[
            pltpu.VMEM((tm, d), F32), pltpu.VMEM((tm, d), F32),
            pltpu.VMEM((tm, d), BF16), pltpu.VMEM((tm, d), BF16),
        ],
        compiler_params=pltpu.CompilerParams(
            dimension_semantics=("arbitrary",), vmem_limit_bytes=V7X_VMEM_LIMIT_BYTES),
        name="out_ffn",
    )(x, fm, yn, gt1, sh2, sc2, x, fm, yn, gt1, sh2, sc2, x, fm, yn, gt1, sh2, sc2, gt2, gt2,
      l1g, l1b, l2g, l2b, wo, wg, wu, wd)


def _pair_lanes(decay, width):
    return jnp.repeat(decay.astype(F32), width).reshape(HEAD_PAIRS, 1, 2 * width)


def kernel(x, c, ctx, c_ctx, w_mod, b_mod, w_in, w_out, decay_fwd, decay_bwd, ln1_g, ln1_b,
           w_ffn_gate, w_ffn_up, w_ffn_down, ln2_g, ln2_b):
    b, n, d = x.shape
    assert w_mod.shape[0] == DEPTH and n % RET_CHUNK == 0 and n % GRID_W == 0
    assert n % (V7X_SUBLANES * DFT_RADIX) == 0
    assert ctx.shape[1] % RET_CHUNK == 0 and w_in.shape[2] == PROJ_WIDTH

    rows = -(-(b + 1) // V7X_SUBLANES) * V7X_SUBLANES
    c_rows = jnp.zeros((rows, d), F32).at[:b].set(c).at[b].set(c_ctx)
    mod = _modulation(c_rows, w_mod[0], b_mod[0])
    sh1, sc1, gt1, sh2, sc2, gt2 = [mod[:b, i * d:(i + 1) * d].reshape(b, 1, d) for i in range(6)]
    csh1 = mod[b:b + 1, 0:d]
    csc1 = mod[b:b + 1, d:2 * d]

    dec_f_k = _pair_lanes(decay_fwd[0], RET_QK_DIM)
    dec_b_k = _pair_lanes(decay_bwd[0], RET_QK_DIM)
    dec_f_v = _pair_lanes(decay_fwd[0], RET_V_DIM)
    dec_b_v = _pair_lanes(decay_bwd[0], RET_V_DIM)

    st_f, st_b, w_in_bf16 = _ctx_states(ctx, csh1, csc1, w_in[0], dec_f_k, dec_b_k)

    cos_t, sin_t = _rope_tables(n)
    w_ch = jnp.asarray(_channel_dft_table(), dtype=F32).astype(BF16)
    qk, vg, fcs, wo, wg, wu, wd = _in_proj(
        x, sh1, sc1, w_in_bf16, w_ch, jnp.asarray(cos_t), jnp.asarray(sin_t),
        (w_out[0], w_ffn_gate[0], w_ffn_up[0], w_ffn_down[0]))

    dft_mat, tw_cos, tw_sin = (jnp.asarray(t, dtype=F32) for t in _position_dft_tables(n))
    fm = _position_dft(dft_mat, tw_cos, tw_sin, fcs)

    yn = _retention(qk, vg, st_f, st_b, dec_f_k, dec_b_k, dec_f_v, dec_b_v)

    return _out_ffn(x, fm, yn, gt1, sh2, sc2, gt2,
                    ln1_g[0].reshape(1, d), ln1_b[0].reshape(1, d), ln2_g[0].reshape(1, d), ln2_b[0].reshape(1, d),
                    wo, wg, wu, wd)
```

```python
import functools

import numpy as np
import jax
import jax.numpy as jnp
from jax import lax
from jax.experimental import pallas as pl
from jax.experimental.pallas import tpu as pltpu

F32 = jnp.float32
BF16 = jnp.bfloat16

GRID_W = 64
FOURIER_GROUPS = 4
FOURIER_GROUP_DIM = 64
FOURIER_WIDTH = FOURIER_GROUPS * FOURIER_GROUP_DIM
RET_HEADS = 6
RET_QK_DIM = 64
RET_V_DIM = 128
RET_QK_WIDTH = RET_HEADS * RET_QK_DIM
RET_WIDTH = RET_HEADS * RET_V_DIM
RET_CHUNK = 128
HEAD_PAIRS = RET_HEADS // 2
PAIR_QK = 2 * RET_QK_DIM
PAIR_V = 2 * RET_V_DIM
ROPE_BASE = 10000.0
LN_EPS = 1e-6
DEPTH = 1
DEEPNORM_ALPHA = (2.0 * DEPTH) ** 0.25

COL_F = 0
COL_Q = COL_F + FOURIER_WIDTH
COL_K = COL_Q + RET_QK_WIDTH
COL_V = COL_K + RET_QK_WIDTH
COL_G = COL_V + RET_WIDTH
PROJ_WIDTH = COL_G + RET_WIDTH

V7X_LANES = 128
V7X_SUBLANES = 8
V7X_VMEM_LIMIT_BYTES = 56 * 1024 * 1024

TOKEN_TILE = 512
IN_PROJ_TILE = 1024
FFN_CHUNK = 256
DFT_RADIX = 4
CTX_BATCH_GROUP = 4
BF16_SUBLANES = 16


def _dot(a, b):
    return jnp.dot(a, b, preferred_element_type=F32)


def _dot_tt(a, b, lhs_dim, rhs_dim):
    return lax.dot_general(a, b, (((lhs_dim,), (rhs_dim,)), ((), ())), preferred_element_type=F32)


def _layer_norm(x):
    mu = jnp.mean(x, axis=-1, keepdims=True)
    xc = x - mu
    var = jnp.mean(xc * xc, axis=-1, keepdims=True)
    return xc * lax.rsqrt(var + LN_EPS)


def _silu(x):
    return x * jax.nn.sigmoid(x)


def _log_sigmoid(x):
    return jnp.minimum(x, 0.0) - jnp.log1p(jnp.exp(-jnp.abs(x)))


@functools.lru_cache(maxsize=None)
def _rope_tables(n_tokens):
    nf = RET_QK_DIM // 4
    t = np.arange(n_tokens)
    row = (t // GRID_W).astype(np.float32)
    col = (t % GRID_W).astype(np.float32)
    freqs = (np.float32(ROPE_BASE) ** (-np.arange(nf, dtype=np.float32) / np.float32(nf))).astype(np.float32)
    ang_r = (row[:, None] * freqs[None, :]).astype(np.float32)
    ang_c = (col[:, None] * freqs[None, :]).astype(np.float32)
    cos_h = np.concatenate([np.cos(ang_r), np.cos(ang_r), np.cos(ang_c), np.cos(ang_c)], axis=1)
    sin_h = np.concatenate([-np.sin(ang_r), np.sin(ang_r), -np.sin(ang_c), np.sin(ang_c)], axis=1)
    reps = V7X_LANES // RET_QK_DIM
    return (np.tile(cos_h, (1, reps)).astype(np.float32), np.tile(sin_h, (1, reps)).astype(np.float32))


@functools.lru_cache(maxsize=None)
def _channel_dft_table():
    m = np.arange(FOURIER_GROUP_DIM)
    ang = 2.0 * np.pi * ((m[:, None] * m[None, :]) % FOURIER_GROUP_DIM) / FOURIER_GROUP_DIM
    scale = FOURIER_GROUP_DIM ** -0.5
    eye = np.eye(FOURIER_GROUPS)
    return np.concatenate([np.kron(eye, np.cos(ang) * scale), np.kron(eye, np.sin(ang) * scale)], axis=1)


@functools.lru_cache(maxsize=None)
def _position_dft_tables(n_tokens):
    sub = n_tokens // DFT_RADIX
    m = np.arange(sub, dtype=np.int64)
    ang = 2.0 * np.pi * ((m[:, None] * m[None, :]) % sub) / sub
    scale = n_tokens ** -0.5
    mat = np.concatenate([np.cos(ang) * scale, -np.sin(ang) * scale], axis=1)
    k1 = np.arange(1, DFT_RADIX, dtype=np.int64)
    phi = 2.0 * np.pi * ((k1[:, None] * m[None, :]) % n_tokens) / n_tokens
    lanes = np.ones((1, 1, V7X_LANES))
    return mat, np.cos(phi)[:, :, None] * lanes, np.sin(phi)[:, :, None] * lanes


def _modulation_kernel(c_ref, w_ref, b_ref, o_ref):
    s = _silu(c_ref[...]).astype(BF16)
    o_ref[...] = _dot(s, w_ref[...].astype(BF16)) + b_ref[...]


def _modulation(c_rows, w_mod, b_mod):
    rows, d = c_rows.shape
    width = w_mod.shape[1]
    tn = width // 4
    return pl.pallas_call(
        _modulation_kernel,
        grid=(width // tn,),
        in_specs=[
            pl.BlockSpec((rows, d), lambda j: (0, 0)),
            pl.BlockSpec((d, tn), lambda j: (0, j)),
            pl.BlockSpec((1, tn), lambda j: (0, j)),
        ],
        out_specs=pl.BlockSpec((rows, tn), lambda j: (0, j)),
        out_shape=jax.ShapeDtypeStruct((rows, width), F32),
        compiler_params=pltpu.CompilerParams(vmem_limit_bytes=V7X_VMEM_LIMIT_BYTES),
        name="modulation",
    )(c_rows, w_mod, b_mod.reshape(1, width))


def _block_diag_mask(shape):
    r = lax.broadcasted_iota(jnp.int32, shape, 0)
    c = lax.broadcasted_iota(jnp.int32, shape, 1)
    return (r >= RET_QK_DIM) == (c >= RET_V_DIM)


def _ctx_states_kernel(ctx_ref, sh_ref, sc_ref, w_ref, dfk_ref, dbk_ref, stf_ref, stb_ref, wq_ref):
    group, length, d = ctx_ref.shape

    @pl.when(pl.program_id(0) == 0)
    def _():
        wq_ref[...] = w_ref[...].astype(BF16)

    x = ctx_ref[...].reshape(group * length, d)
    u = (_layer_norm(x) * (1.0 + sc_ref[...]) + sh_ref[...]).astype(BF16)
    kv = _dot(u, w_ref[:, COL_K:COL_G].astype(BF16))
    k = kv[:, :RET_QK_WIDTH] * (RET_QK_DIM ** -0.5)
    v = kv[:, RET_QK_WIDTH:].astype(BF16)
    t = lax.broadcasted_iota(jnp.int32, (length, PAIR_QK), 0).astype(F32)
    bd = _block_diag_mask((PAIR_QK, PAIR_V))
    for p in range(HEAD_PAIRS):
        w_f = jnp.exp(_log_sigmoid(dfk_ref[p]) * (length - 1.0 - t))
        w_b = jnp.exp(_log_sigmoid(dbk_ref[p]) * t)
        for bi in range(group):
            rows = slice(bi * length, (bi + 1) * length)
            k2 = k[rows, p * PAIR_QK:(p + 1) * PAIR_QK]
            v2 = v[rows, p * PAIR_V:(p + 1) * PAIR_V]
            st_f = _dot_tt((k2 * w_f).astype(BF16), v2, 0, 0)
            st_b = _dot_tt((k2 * w_b).astype(BF16), v2, 0, 0)
            stf_ref[bi, p] = jnp.where(bd, st_f, 0.0)
            stb_ref[bi, p] = jnp.where(bd, st_b, 0.0)


def _ctx_states(ctx, csh, csc, w_in, dec_f_k, dec_b_k):
    b, length, d = ctx.shape
    group = CTX_BATCH_GROUP if b % CTX_BATCH_GROUP == 0 else 1
    st_shape = jax.ShapeDtypeStruct((b, HEAD_PAIRS, PAIR_QK, PAIR_V), F32)
    st_spec = pl.BlockSpec((group, HEAD_PAIRS, PAIR_QK, PAIR_V), lambda i: (i, 0, 0, 0))
    dec_spec = pl.BlockSpec((HEAD_PAIRS, 1, PAIR_QK), lambda i: (0, 0, 0))
    return pl.pallas_call(
        _ctx_states_kernel,
        grid=(b // group,),
        in_specs=[
            pl.BlockSpec((group, length, d), lambda i: (i, 0, 0)),
            pl.BlockSpec((1, d), lambda i: (0, 0)),
            pl.BlockSpec((1, d), lambda i: (0, 0)),
            pl.BlockSpec((d, PROJ_WIDTH), lambda i: (0, 0), pipeline_mode=pl.Buffered(1)),
            dec_spec,
            dec_spec,
        ],
        out_specs=[st_spec, st_spec, pl.BlockSpec((d, PROJ_WIDTH), lambda i: (0, 0))],
        out_shape=[st_shape, st_shape, jax.ShapeDtypeStruct((d, PROJ_WIDTH), BF16)],
        compiler_params=pltpu.CompilerParams(
            dimension_semantics=("arbitrary",), vmem_limit_bytes=V7X_VMEM_LIMIT_BYTES),
        name="ctx_states",
    )(ctx, csh, csc, w_in, dec_f_k, dec_b_k)


def _rope(x, cos, sin_signed, first_of_pair):
    nf = RET_QK_DIM // 4
    outs = []
    for j in range(x.shape[1] // V7X_LANES):
        xj = x[:, j * V7X_LANES:(j + 1) * V7X_LANES]
        partner = jnp.where(first_of_pair, pltpu.roll(xj, V7X_LANES - nf, 1), pltpu.roll(xj, nf, 1))
        outs.append(xj * cos + partner * sin_signed)
    return jnp.concatenate(outs, axis=1)


def _tile_specs(tile_of_step, tiles_per_batch, tm, d, **kw):
    bi = lambda s: tile_of_step(s) // tiles_per_batch
    ti = lambda s: tile_of_step(s) % tiles_per_batch
    tok = lambda width: pl.BlockSpec((1, tm, width), lambda s: (bi(s), ti(s), 0), **kw)
    vec = pl.BlockSpec((1, 1, d), lambda s: (bi(s), 0, 0), **kw)
    four = pl.BlockSpec((tm, FOURIER_WIDTH), lambda s: (ti(s), bi(s)), **kw)
    rope = pl.BlockSpec((tm, V7X_LANES), lambda s: (ti(s), 0), **kw)
    return tok, vec, four, rope


def _in_proj_kernel(x0_ref, sh0_ref, sc0_ref, xn_ref, shn_ref, scn_ref, w_ref, wch_ref, cos_ref, sin_ref,
                    wo_ref, wg_ref, wu_ref, wd_ref,
                    qk_ref, vg_ref, fcs_ref, wo_out, wg_out, wu_out, wd_out, u_a, u_b):
    step = pl.program_id(0)
    for src, dst in ((wo_ref, wo_out), (wg_ref, wg_out), (wu_ref, wu_out), (wd_ref, wd_out)):
        dst[...] = src[...].astype(BF16)

    def prepare(x_ref, sh_ref, sc_ref, u_dst):
        u_dst[...] = (_layer_norm(x_ref[0]) * (1.0 + sc_ref[0]) + sh_ref[0]).astype(BF16)

    def project(u_src):
        p = _dot(u_src[...], w_ref[...])
        f = p[:, COL_F:COL_Q].astype(BF16)
        fcs_ref[0] = _dot(f, wch_ref[...]).astype(BF16)

        cos = cos_ref[...]
        sin_signed = sin_ref[...]
        lane = lax.broadcasted_iota(jnp.int32, cos.shape, 1)
        first_of_pair = jnp.bitwise_and(lane, 2 * (RET_QK_DIM // 4) - 1) < (RET_QK_DIM // 4)
        qk_ref[0, :, :RET_QK_WIDTH] = _rope(p[:, COL_Q:COL_K], cos, sin_signed, first_of_pair).astype(BF16)
        k = p[:, COL_K:COL_V] * (RET_QK_DIM ** -0.5)
        qk_ref[0, :, RET_QK_WIDTH:] = _rope(k, cos, sin_signed, first_of_pair).astype(BF16)
        vg_ref[0, :, :RET_WIDTH] = p[:, COL_V:COL_G].astype(BF16)
        vg_ref[0, :, RET_WIDTH:] = _silu(p[:, COL_G:PROJ_WIDTH]).astype(BF16)

    @pl.when(step == 0)
    def _():
        prepare(x0_ref, sh0_ref, sc0_ref, u_a)

    even = lax.rem(step, 2) == 0

    @pl.when(even)
    def _():
        prepare(xn_ref, shn_ref, scn_ref, u_b)
        project(u_a)

    @pl.when(jnp.logical_not(even))
    def _():
        prepare(xn_ref, shn_ref, scn_ref, u_a)
        project(u_b)


def _cast_block_rows(rows, n_steps):
    for r in range(BF16_SUBLANES, rows + 1, BF16_SUBLANES):
        if rows % r == 0 and rows // r <= n_steps:
            return r
    raise ValueError("no aligned row blocking for the weight cast")


def _in_proj(x, sh1, sc1, w_in_bf16, w_ch, cos_t, sin_t, ffn_weights):
    b, n, d = x.shape
    tm = min(IN_PROJ_TILE, n)
    tiles_per_batch = n // tm
    n_tiles = b * tiles_per_batch
    tok0, vec0, _, _ = _tile_specs(lambda s: 0 * s, tiles_per_batch, tm, d, pipeline_mode=pl.Buffered(1))
    tokn, vecn, _, _ = _tile_specs(lambda s: jnp.minimum(s + 1, n_tiles - 1), tiles_per_batch, tm, d)
    tok, _, _, rope = _tile_specs(lambda s: s, tiles_per_batch, tm, d)
    resident = lambda shape: pl.BlockSpec(shape, lambda s: (0, 0), pipeline_mode=pl.Buffered(1))

    cast_specs = []
    for w in ffn_weights:
        r = _cast_block_rows(w.shape[0], n_tiles)
        last = w.shape[0] // r - 1
        cast_specs.append(pl.BlockSpec((r, w.shape[1]), lambda s, last=last: (jnp.minimum(s, last), 0)))

    widths = (2 * RET_QK_WIDTH, 2 * RET_WIDTH, 2 * FOURIER_WIDTH)
    return pl.pallas_call(
        _in_proj_kernel,
        grid=(n_tiles,),
        in_specs=[
            tok0(d), vec0, vec0, tokn(d), vecn, vecn,
            resident((d, PROJ_WIDTH)), resident((FOURIER_WIDTH, 2 * FOURIER_WIDTH)),
            rope, rope, *cast_specs,
        ],
        out_specs=[tok(width) for width in widths] + cast_specs,
        out_shape=[jax.ShapeDtypeStruct((b, n, width), BF16) for width in widths]
        + [jax.ShapeDtypeStruct(w.shape, BF16) for w in ffn_weights],
        scratch_shapes=[pltpu.VMEM((tm, d), BF16), pltpu.VMEM((tm, d), BF16)],
        compiler_params=pltpu.CompilerParams(
            dimension_semantics=("arbitrary",), vmem_limit_bytes=V7X_VMEM_LIMIT_BYTES),
        name="in_proj",
    )(x, sh1, sc1, x, sh1, sc1, w_in_bf16, w_ch, cos_t, sin_t, *ffn_weights)


def _position_dft_sequence(fcs_ref, mat_ref, twc_ref, tws_ref, o_ref, bi):
    sub = mat_ref.shape[0]
    width = fcs_ref.shape[2] // 2
    reps = width // V7X_LANES
    mat = mat_ref[...].astype(BF16)
    c0, c1, c2, c3 = (fcs_ref[bi, j * sub:(j + 1) * sub, :width].astype(F32) for j in range(DFT_RADIX))
    s0, s1, s2, s3 = (fcs_ref[bi, j * sub:(j + 1) * sub, width:].astype(F32) for j in range(DFT_RADIX))
    tc0, tc1, tc2, tc3 = c0 + c2, c1 + c3, c0 - c2, c1 - c3
    ts0, ts1, ts2, ts3 = s0 + s2, s1 + s3, s0 - s2, s1 - s3
    parts = [
        (tc0 + tc1, ts0 + ts1),
        (tc2 - ts3, ts2 + tc3),
        (tc0 - tc1, ts0 - ts1),
        (tc2 + ts3, ts2 - tc3),
    ]
    for k1, (a, b) in enumerate(parts):
        if k1 > 0:
            cphi = jnp.concatenate([twc_ref[k1 - 1]] * reps, axis=1)
            sphi = jnp.concatenate([tws_ref[k1 - 1]] * reps, axis=1)
            a, b = a * cphi - b * sphi, a * sphi + b * cphi
        rhs = jnp.concatenate([a.astype(BF16), b.astype(BF16)], axis=0)
        y = _dot(mat, rhs)
        for h in range(reps):
            o_ref[bi, h, pl.ds(k1, sub, stride=DFT_RADIX), :] = y[:, h * V7X_LANES:(h + 1) * V7X_LANES]


def _mixer_kernel(qk_ref, vg_ref, stf0_ref, stb0_ref, dfk_ref, dbk_ref, dfv_ref, dbv_ref,
                  fcs_ref, mat_ref, twc_ref, tws_ref,
                  o_ref, fm_ref, u_all, s_all, tab_s, xi_s):
    _position_dft_sequence(fcs_ref, mat_ref, twc_ref, tws_ref, fm_ref, 0)

    n = qk_ref.shape[1]
    n_chunks = n // RET_CHUNK
    c = RET_CHUNK

    i_k = lax.broadcasted_iota(jnp.int32, (c, PAIR_QK), 0).astype(F32)
    i_v = lax.broadcasted_iota(jnp.int32, (c, PAIR_V), 0)
    j_v = jnp.bitwise_and(lax.broadcasted_iota(jnp.int32, (c, PAIR_V), 1), RET_V_DIM - 1)
    diff = (i_v - j_v).astype(F32)
    bd = _block_diag_mask((PAIR_QK, PAIR_V))
    lane_k = lax.broadcasted_iota(jnp.int32, (c, PAIR_QK), 1)
    lane_v = lax.broadcasted_iota(jnp.int32, (c, PAIR_V), 1)
    zero_k = jnp.zeros((c, PAIR_QK), BF16)
    zero_v = jnp.zeros((c, PAIR_V), BF16)

    for p in range(HEAD_PAIRS):
        rows = lambda i: slice(i * c, (i + 1) * c)
        q_chunk = lambda i: qk_ref[0, rows(i), p * PAIR_QK:(p + 1) * PAIR_QK]
        k_chunk = lambda i: qk_ref[0, rows(i), RET_QK_WIDTH + p * PAIR_QK:RET_QK_WIDTH + (p + 1) * PAIR_QK]
        v_chunk = lambda i: vg_ref[0, rows(i), p * PAIR_V:(p + 1) * PAIR_V]
        g_chunk = lambda i: vg_ref[0, rows(i), RET_WIDTH + p * PAIR_V:RET_WIDTH + (p + 1) * PAIR_V]

        lgf_k = _log_sigmoid(dfk_ref[p])
        lgb_k = _log_sigmoid(dbk_ref[p])
        lgf_v = _log_sigmoid(dfv_ref[p])
        lgb_v = _log_sigmoid(dbv_ref[p])

        tab_s[p] = jnp.where(diff >= 0.0, jnp.exp(lgf_v * jnp.maximum(diff, 0.0)),
                             jnp.exp(lgb_v * jnp.maximum(-diff, 0.0)))
        xi_s[p, 0] = jnp.exp(lgf_k * (i_k + 1.0))
        xi_s[p, 1] = jnp.exp(lgb_k * (c - i_k))
        g_f = jnp.exp(lgf_v * float(c))
        g_b = jnp.exp(lgb_v * float(c))
        zeta_f = jnp.exp(lgf_k * (c - 1.0 - i_k))
        zeta_b = jnp.exp(lgb_k * i_k)

        for i in range(n_chunks):
            k2 = k_chunk(i).astype(F32)
            kz = jnp.concatenate([(k2 * zeta_f).astype(BF16), (k2 * zeta_b).astype(BF16)], axis=1)
            u_all[p, i] = _dot_tt(kz, v_chunk(i), 0, 0)

        s_f = stf0_ref[0, p]
        s_b = stb0_ref[0, p]
        for i in range(n_chunks):
            j = n_chunks - 1 - i
            s_all[p, i, :PAIR_QK, :] = jnp.where(bd, s_f, 0.0).astype(BF16)
            s_all[p, j, PAIR_QK:, :] = jnp.where(bd, s_b, 0.0).astype(BF16)
            if i + 1 < n_chunks:
                s_f = g_f * s_f + u_all[p, i, :PAIR_QK, :]
                s_b = g_b * s_b + u_all[p, j, PAIR_QK:, :]

        for i in range(n_chunks):
            q2 = q_chunk(i)
            k2 = k_chunk(i)
            v2 = v_chunk(i)
            k_bd = jnp.concatenate([jnp.where(lane_k < RET_QK_DIM, k2, zero_k),
                                    jnp.where(lane_k >= RET_QK_DIM, k2, zero_k)], axis=0)
            v_bd = jnp.concatenate([jnp.where(lane_v < RET_V_DIM, v2, zero_v),
                                    jnp.where(lane_v >= RET_V_DIM, v2, zero_v)], axis=0)
            scores = _dot_tt(q2, k_bd, 1, 1)
            pr = (scores * tab_s[p]).astype(BF16)
            q2f = q2.astype(F32)
            qx = jnp.concatenate([(q2f * xi_s[p, 0]).astype(BF16), (q2f * xi_s[p, 1]).astype(BF16)], axis=1)
            y = _dot(pr, v_bd) + _dot(qx, s_all[p, i])
            g = g_chunk(i).astype(F32)
            outs = []
            for h in range(2):
                yh = y[:, h * RET_V_DIM:(h + 1) * RET_V_DIM]
                ms = jnp.mean(yh * yh, axis=-1, keepdims=True)
                outs.append(yh * lax.rsqrt(ms + LN_EPS) * g[:, h * RET_V_DIM:(h + 1) * RET_V_DIM])
            o_ref[0, rows(i), p * PAIR_V:(p + 1) * PAIR_V] = jnp.concatenate(outs, axis=1).astype(BF16)


def _mixer(qk, vg, fcs, st_f, st_b, dec_f_k, dec_b_k, dec_f_v, dec_b_v, mat, tw_cos, tw_sin):
    b, n, _ = qk.shape
    n_chunks = n // RET_CHUNK
    planes = fcs.shape[2] // 2 // V7X_LANES
    seq = lambda width: pl.BlockSpec((1, n, width), lambda bi: (bi, 0, 0))
    st = pl.BlockSpec((1, HEAD_PAIRS, PAIR_QK, PAIR_V), lambda bi: (bi, 0, 0, 0))
    dk = pl.BlockSpec((HEAD_PAIRS, 1, PAIR_QK), lambda bi: (0, 0, 0))
    dv = pl.BlockSpec((HEAD_PAIRS, 1, PAIR_V), lambda bi: (0, 0, 0))
    resident = lambda shape: pl.BlockSpec(shape, lambda bi: (0,) * len(shape), pipeline_mode=pl.Buffered(1))
    return pl.pallas_call(
        _mixer_kernel,
        grid=(b,),
        in_specs=[seq(2 * RET_QK_WIDTH), seq(2 * RET_WIDTH), st, st, dk, dk, dv, dv,
                  seq(fcs.shape[2]), resident(mat.shape), resident(tw_cos.shape), resident(tw_sin.shape)],
        out_specs=[seq(RET_WIDTH), pl.BlockSpec((1, planes, n, V7X_LANES), lambda bi: (bi, 0, 0, 0))],
        out_shape=[jax.ShapeDtypeStruct((b, n, RET_WIDTH), BF16),
                   jax.ShapeDtypeStruct((b, planes, n, V7X_LANES), F32)],
        scratch_shapes=[
            pltpu.VMEM((HEAD_PAIRS, n_chunks, 2 * PAIR_QK, PAIR_V), F32),
            pltpu.VMEM((HEAD_PAIRS, n_chunks, 2 * PAIR_QK, PAIR_V), BF16),
            pltpu.VMEM((HEAD_PAIRS, RET_CHUNK, PAIR_V), F32),
            pltpu.VMEM((HEAD_PAIRS, 2, RET_CHUNK, PAIR_QK), F32),
        ],
        compiler_params=pltpu.CompilerParams(vmem_limit_bytes=V7X_VMEM_LIMIT_BYTES),
        name="mixer",
    )(qk, vg, st_f, st_b, dec_f_k, dec_b_k, dec_f_v, dec_b_v, fcs, mat, tw_cos, tw_sin)


def _out_ffn_kernel(x0_ref, fm0_ref, yn0_ref, gt10_ref, sh20_ref, sc20_ref,
                    xb_ref, fmb_ref, ynb_ref, gt1b_ref, sh2b_ref, sc2b_ref,
                    xa_ref, fma_ref, yna_ref, gt1a_ref, sh2a_ref, sc2a_ref,
                    gt2a_ref, gt2b_ref, l1g_ref, l1b_ref, l2g_ref, l2b_ref, wo_ref, wg_ref, wu_ref, wd_ref,
                    o_ref, x1_a, x1_b, u2_a, u2_b):
    tm = x1_a.shape[0]

    def prepare(x_ref, fm_ref, yn_ref, gt1_ref, sh2_ref, sc2_ref, x1_dst, u2_dst):
        fm = jnp.concatenate([fm_ref[0, h] for h in range(fm_ref.shape[1])], axis=1).astype(BF16)
        mix = _dot(fm, wo_ref[:FOURIER_WIDTH, :]) + _dot(yn_ref[0], wo_ref[FOURIER_WIDTH:, :])
        x1 = _layer_norm(DEEPNORM_ALPHA * x_ref[0] + gt1_ref[0] * mix) * l1g_ref[...] + l1b_ref[...]
        x1_dst[...] = x1
        u2_dst[...] = (_layer_norm(x1) * (1.0 + sc2_ref[0]) + sh2_ref[0]).astype(BF16)

    def ffn(x1_src, u2_src, gt2_ref, row0):
        d_ff = wg_ref.shape[1]
        acc = None
        for c0 in range(0, d_ff, FFN_CHUNK):
            c1 = c0 + FFN_CHUNK
            u2 = u2_src[...]
            h = (_silu(_dot(u2, wg_ref[:, c0:c1])) * _dot(u2, wu_ref[:, c0:c1])).astype(BF16)
            part = _dot(h, wd_ref[c0:c1, :])
            acc = part if acc is None else acc + part
        z = DEEPNORM_ALPHA * x1_src[...] + gt2_ref[0] * acc
        o_ref[0, row0:row0 + tm, :] = _layer_norm(z) * l2g_ref[...] + l2b_ref[...]

    @pl.when(pl.program_id(0) == 0)
    def _():
        prepare(x0_ref, fm0_ref, yn0_ref, gt10_ref, sh20_ref, sc20_ref, x1_a, u2_a)

    prepare(xb_ref, fmb_ref, ynb_ref, gt1b_ref, sh2b_ref, sc2b_ref, x1_b, u2_b)
    ffn(x1_a, u2_a, gt2a_ref, 0)
    prepare(xa_ref, fma_ref, yna_ref, gt1a_ref, sh2a_ref, sc2a_ref, x1_a, u2_a)
    ffn(x1_b, u2_b, gt2b_ref, tm)


def _out_ffn(x, fm, yn, gt1, sh2, sc2, gt2, l1g, l1b, l2g, l2b, wo, wg, wu, wd):
    b, n, d = x.shape
    d_ff = wg.shape[1]
    assert d_ff % FFN_CHUNK == 0
    tm = min(TOKEN_TILE, n // 2)
    tiles_per_batch = n // tm
    n_tiles = b * tiles_per_batch
    assert tiles_per_batch % 2 == 0
    planes = fm.shape[1]

    def specs(tile_of_step, **kw):
        tok, vec, _, _ = _tile_specs(tile_of_step, tiles_per_batch, tm, d, **kw)
        bi = lambda s: tile_of_step(s) // tiles_per_batch
        ti = lambda s: tile_of_step(s) % tiles_per_batch
        four = pl.BlockSpec((1, planes, tm, V7X_LANES), lambda s: (bi(s), 0, ti(s), 0), **kw)
        return tok, vec, four

    tok0, vec0, four0 = specs(lambda g: 0 * g, pipeline_mode=pl.Buffered(1))
    tokb, vecb, fourb = specs(lambda g: 2 * g + 1)
    toka, veca, foura = specs(lambda g: jnp.minimum(2 * g + 2, n_tiles - 1))
    _, vecc, _ = specs(lambda g: 2 * g)
    par = pl.BlockSpec((1, d), lambda g: (0, 0))
    resident = lambda shape: pl.BlockSpec(shape, lambda g: (0, 0), pipeline_mode=pl.Buffered(1))
    pairs_per_batch = tiles_per_batch // 2
    out_spec = pl.BlockSpec((1, 2 * tm, d), lambda g: (g // pairs_per_batch, g % pairs_per_batch, 0))
    return pl.pallas_call(
        _out_ffn_kernel,
        grid=(n_tiles // 2,),
        in_specs=[
            tok0(d), four0, tok0(RET_WIDTH), vec0, vec0, vec0,
            tokb(d), fourb, tokb(RET_WIDTH), vecb, vecb, vecb,
            toka(d), foura, toka(RET_WIDTH), veca, veca, veca,
            vecc, vecb, par, par, par, par,
            resident((d, d)), resident((d, d_ff)), resident((d, d_ff)), resident((d_ff, d)),
        ],
        out_specs=out_spec,
        out_shape=jax.ShapeDtypeStruct((b, n, d), x.dtype),
        scratch_shapes=[
            pltpu.VMEM((tm, d), F32), pltpu.VMEM((tm, d), F32),
            pltpu.VMEM((tm, d), BF16), pltpu.VMEM((tm, d), BF16),
        ],
        compiler_params=pltpu.CompilerParams(
            dimension_semantics=("arbitrary",), vmem_limit_bytes=V7X_VMEM_LIMIT_BYTES),
        name="out_ffn",
    )(x, fm, yn, gt1, sh2, sc2, x, fm, yn, gt1, sh2, sc2, x, fm, yn, gt1, sh2, sc2, gt2, gt2,
      l1g, l1b, l2g, l2b, wo, wg, wu, wd)


def _pair_lanes(decay, width):
    return jnp.repeat(decay.astype(F32), width).reshape(HEAD_PAIRS, 1, 2 * width)


def kernel(x, c, ctx, c_ctx, w_mod, b_mod, w_in, w_out, decay_fwd, decay_bwd, ln1_g, ln1_b,
           w_ffn_gate, w_ffn_up, w_ffn_down, ln2_g, ln2_b):
    b, n, d = x.shape
    assert w_mod.shape[0] == DEPTH and n % RET_CHUNK == 0 and n % GRID_W == 0
    assert n % (V7X_SUBLANES * DFT_RADIX) == 0
    assert ctx.shape[1] % RET_CHUNK == 0 and w_in.shape[2] == PROJ_WIDTH

    rows = -(-(b + 1) // V7X_SUBLANES) * V7X_SUBLANES
    c_rows = jnp.zeros((rows, d), F32).at[:b].set(c).at[b].set(c_ctx)
    mod = _modulation(c_rows, w_mod[0], b_mod[0])
    sh1, sc1, gt1, sh2, sc2, gt2 = [mod[:b, i * d:(i + 1) * d].reshape(b, 1, d) for i in range(6)]
    csh1 = mod[b:b + 1, 0:d]
    csc1 = mod[b:b + 1, d:2 * d]

    dec_f_k = _pair_lanes(decay_fwd[0], RET_QK_DIM)
    dec_b_k = _pair_lanes(decay_bwd[0], RET_QK_DIM)
    dec_f_v = _pair_lanes(decay_fwd[0], RET_V_DIM)
    dec_b_v = _pair_lanes(decay_bwd[0], RET_V_DIM)

    st_f, st_b, w_in_bf16 = _ctx_states(ctx, csh1, csc1, w_in[0], dec_f_k, dec_b_k)

    cos_t, sin_t = _rope_tables(n)
    w_ch = jnp.asarray(_channel_dft_table(), dtype=F32).astype(BF16)
    qk, vg, fcs, wo, wg, wu, wd = _in_proj(
        x, sh1, sc1, w_in_bf16, w_ch, jnp.asarray(cos_t), jnp.asarray(sin_t),
        (w_out[0], w_ffn_gate[0], w_ffn_up[0], w_ffn_down[0]))

    dft_mat, tw_cos, tw_sin = (jnp.asarray(t, dtype=F32) for t in _position_dft_tables(n))
    yn, fm = _mixer(qk, vg, fcs, st_f, st_b, dec_f_k, dec_b_k, dec_f_v, dec_b_v, dft_mat, tw_cos, tw_sin)

    return _out_ffn(x, fm, yn, gt1, sh2, sc2, gt2,
                    ln1_g[0].reshape(1, d), ln1_b[0].reshape(1, d), ln2_g[0].reshape(1, d), ln2_b[0].reshape(1, d),
                    wo, wg, wu, wd)
```

```python
import functools

import numpy as np
import jax
import jax.numpy as jnp
from jax import lax
from jax.experimental import pallas as pl
from jax.experimental.pallas import tpu as pltpu

F32 = jnp.float32
BF16 = jnp.bfloat16

GRID_W = 64
FOURIER_GROUPS = 4
FOURIER_GROUP_DIM = 64
FOURIER_WIDTH = FOURIER_GROUPS * FOURIER_GROUP_DIM
RET_HEADS = 6
RET_QK_DIM = 64
RET_V_DIM = 128
RET_QK_WIDTH = RET_HEADS * RET_QK_DIM
RET_WIDTH = RET_HEADS * RET_V_DIM
RET_CHUNK = 128
HEAD_PAIRS = RET_HEADS // 2
PAIR_QK = 2 * RET_QK_DIM
PAIR_V = 2 * RET_V_DIM
ROPE_BASE = 10000.0
LN_EPS = 1e-6
DEPTH = 1
DEEPNORM_ALPHA = (2.0 * DEPTH) ** 0.25

COL_F = 0
COL_Q = COL_F + FOURIER_WIDTH
COL_K = COL_Q + RET_QK_WIDTH
COL_V = COL_K + RET_QK_WIDTH
COL_G = COL_V + RET_WIDTH
PROJ_WIDTH = COL_G + RET_WIDTH

V7X_LANES = 128
V7X_SUBLANES = 8
V7X_VMEM_LIMIT_BYTES = 56 * 1024 * 1024

TOKEN_TILE = 512
IN_PROJ_TILE = 512
FFN_CHUNK = 256
DFT_RADIX = 4
CTX_BATCH_GROUP = 4
BF16_SUBLANES = 16


def _dot(a, b):
    return jnp.dot(a, b, preferred_element_type=F32)


def _dot_tt(a, b, lhs_dim, rhs_dim):
    return lax.dot_general(a, b, (((lhs_dim,), (rhs_dim,)), ((), ())), preferred_element_type=F32)


def _layer_norm(x):
    mu = jnp.mean(x, axis=-1, keepdims=True)
    xc = x - mu
    var = jnp.mean(xc * xc, axis=-1, keepdims=True)
    return xc * lax.rsqrt(var + LN_EPS)


def _silu(x):
    return x * jax.nn.sigmoid(x)


def _log_sigmoid(x):
    return jnp.minimum(x, 0.0) - jnp.log1p(jnp.exp(-jnp.abs(x)))


@functools.lru_cache(maxsize=None)
def _rope_tables(n_tokens):
    nf = RET_QK_DIM // 4
    t = np.arange(n_tokens)
    row = (t // GRID_W).astype(np.float32)
    col = (t % GRID_W).astype(np.float32)
    freqs = (np.float32(ROPE_BASE) ** (-np.arange(nf, dtype=np.float32) / np.float32(nf))).astype(np.float32)
    ang_r = (row[:, None] * freqs[None, :]).astype(np.float32)
    ang_c = (col[:, None] * freqs[None, :]).astype(np.float32)
    cos_h = np.concatenate([np.cos(ang_r), np.cos(ang_r), np.cos(ang_c), np.cos(ang_c)], axis=1)
    sin_h = np.concatenate([-np.sin(ang_r), np.sin(ang_r), -np.sin(ang_c), np.sin(ang_c)], axis=1)
    reps = V7X_LANES // RET_QK_DIM
    return (np.tile(cos_h, (1, reps)).astype(np.float32), np.tile(sin_h, (1, reps)).astype(np.float32))


@functools.lru_cache(maxsize=None)
def _channel_dft_table():
    m = np.arange(FOURIER_GROUP_DIM)
    ang = 2.0 * np.pi * ((m[:, None] * m[None, :]) % FOURIER_GROUP_DIM) / FOURIER_GROUP_DIM
    scale = FOURIER_GROUP_DIM ** -0.5
    eye = np.eye(FOURIER_GROUPS)
    return np.concatenate([np.kron(eye, np.cos(ang) * scale), np.kron(eye, np.sin(ang) * scale)], axis=1)


@functools.lru_cache(maxsize=None)
def _position_dft_tables(n_tokens):
    sub = n_tokens // DFT_RADIX
    m = np.arange(sub, dtype=np.int64)
    ang = 2.0 * np.pi * ((m[:, None] * m[None, :]) % sub) / sub
    scale = n_tokens ** -0.5
    mat = np.concatenate([np.cos(ang) * scale, -np.sin(ang) * scale], axis=1)
    k1 = np.arange(1, DFT_RADIX, dtype=np.int64)
    phi = 2.0 * np.pi * ((k1[:, None] * m[None, :]) % n_tokens) / n_tokens
    lanes = np.ones((1, 1, V7X_LANES))
    return mat, np.cos(phi)[:, :, None] * lanes, np.sin(phi)[:, :, None] * lanes


def _modulation_kernel(c_ref, w_ref, b_ref, o_ref):
    s = _silu(c_ref[...]).astype(BF16)
    o_ref[...] = _dot(s, w_ref[...].astype(BF16)) + b_ref[...]


def _modulation(c_rows, w_mod, b_mod):
    rows, d = c_rows.shape
    width = w_mod.shape[1]
    tn = width // 4
    return pl.pallas_call(
        _modulation_kernel,
        grid=(width // tn,),
        in_specs=[
            pl.BlockSpec((rows, d), lambda j: (0, 0)),
            pl.BlockSpec((d, tn), lambda j: (0, j)),
            pl.BlockSpec((1, tn), lambda j: (0, j)),
        ],
        out_specs=pl.BlockSpec((rows, tn), lambda j: (0, j)),
        out_shape=jax.ShapeDtypeStruct((rows, width), F32),
        compiler_params=pltpu.CompilerParams(vmem_limit_bytes=V7X_VMEM_LIMIT_BYTES),
        name="modulation",
    )(c_rows, w_mod, b_mod.reshape(1, width))


def _block_diag_mask(shape):
    r = lax.broadcasted_iota(jnp.int32, shape, 0)
    c = lax.broadcasted_iota(jnp.int32, shape, 1)
    return (r >= RET_QK_DIM) == (c >= RET_V_DIM)


def _ctx_states_kernel(ctx_ref, sh_ref, sc_ref, w_ref, dfk_ref, dbk_ref, stf_ref, stb_ref, wq_ref):
    group, length, d = ctx_ref.shape

    @pl.when(pl.program_id(0) == 0)
    def _():
        wq_ref[...] = w_ref[...].astype(BF16)

    x = ctx_ref[...].reshape(group * length, d)
    u = (_layer_norm(x) * (1.0 + sc_ref[...]) + sh_ref[...]).astype(BF16)
    kv = _dot(u, w_ref[:, COL_K:COL_G].astype(BF16))
    k = kv[:, :RET_QK_WIDTH] * (RET_QK_DIM ** -0.5)
    v = kv[:, RET_QK_WIDTH:].astype(BF16)
    t = lax.broadcasted_iota(jnp.int32, (length, PAIR_QK), 0).astype(F32)
    bd = _block_diag_mask((PAIR_QK, PAIR_V))
    for p in range(HEAD_PAIRS):
        w_f = jnp.exp(_log_sigmoid(dfk_ref[p]) * (length - 1.0 - t))
        w_b = jnp.exp(_log_sigmoid(dbk_ref[p]) * t)
        for bi in range(group):
            rows = slice(bi * length, (bi + 1) * length)
            k2 = k[rows, p * PAIR_QK:(p + 1) * PAIR_QK]
            v2 = v[rows, p * PAIR_V:(p + 1) * PAIR_V]
            st_f = _dot_tt((k2 * w_f).astype(BF16), v2, 0, 0)
            st_b = _dot_tt((k2 * w_b).astype(BF16), v2, 0, 0)
            stf_ref[bi, p] = jnp.where(bd, st_f, 0.0)
            stb_ref[bi, p] = jnp.where(bd, st_b, 0.0)


def _ctx_states(ctx, csh, csc, w_in, dec_f_k, dec_b_k):
    b, length, d = ctx.shape
    group = CTX_BATCH_GROUP if b % CTX_BATCH_GROUP == 0 else 1
    st_shape = jax.ShapeDtypeStruct((b, HEAD_PAIRS, PAIR_QK, PAIR_V), F32)
    st_spec = pl.BlockSpec((group, HEAD_PAIRS, PAIR_QK, PAIR_V), lambda i: (i, 0, 0, 0))
    dec_spec = pl.BlockSpec((HEAD_PAIRS, 1, PAIR_QK), lambda i: (0, 0, 0))
    return pl.pallas_call(
        _ctx_states_kernel,
        grid=(b // group,),
        in_specs=[
            pl.BlockSpec((group, length, d), lambda i: (i, 0, 0)),
            pl.BlockSpec((1, d), lambda i: (0, 0)),
            pl.BlockSpec((1, d), lambda i: (0, 0)),
            pl.BlockSpec((d, PROJ_WIDTH), lambda i: (0, 0), pipeline_mode=pl.Buffered(1)),
            dec_spec,
            dec_spec,
        ],
        out_specs=[st_spec, st_spec, pl.BlockSpec((d, PROJ_WIDTH), lambda i: (0, 0))],
        out_shape=[st_shape, st_shape, jax.ShapeDtypeStruct((d, PROJ_WIDTH), BF16)],
        compiler_params=pltpu.CompilerParams(
            dimension_semantics=("arbitrary",), vmem_limit_bytes=V7X_VMEM_LIMIT_BYTES),
        name="ctx_states",
    )(ctx, csh, csc, w_in, dec_f_k, dec_b_k)


def _rope(x, cos, sin_signed, first_of_pair):
    nf = RET_QK_DIM // 4
    outs = []
    for j in range(x.shape[1] // V7X_LANES):
        xj = x[:, j * V7X_LANES:(j + 1) * V7X_LANES]
        partner = jnp.where(first_of_pair, pltpu.roll(xj, V7X_LANES - nf, 1), pltpu.roll(xj, nf, 1))
        outs.append(xj * cos + partner * sin_signed)
    return jnp.concatenate(outs, axis=1)


def _tile_specs(tile_of_step, tiles_per_batch, tm, d, **kw):
    bi = lambda s: tile_of_step(s) // tiles_per_batch
    ti = lambda s: tile_of_step(s) % tiles_per_batch
    tok = lambda width: pl.BlockSpec((1, tm, width), lambda s: (bi(s), ti(s), 0), **kw)
    vec = pl.BlockSpec((1, 1, d), lambda s: (bi(s), 0, 0), **kw)
    rope = pl.BlockSpec((tm, V7X_LANES), lambda s: (ti(s), 0), **kw)
    return tok, vec, rope


def _in_proj_kernel(x0_ref, sh0_ref, sc0_ref, xn_ref, shn_ref, scn_ref, w_ref, wch_ref, cos_ref, sin_ref,
                    wo_ref, wg_ref, wu_ref, wd_ref,
                    qk_ref, vg_ref, fcs_ref, wo_out, wg_out, wu_out, wd_out, u_a, u_b):
    step = pl.program_id(0)
    for src, dst in ((wo_ref, wo_out), (wg_ref, wg_out), (wu_ref, wu_out), (wd_ref, wd_out)):
        dst[...] = src[...].astype(BF16)

    def prepare(x_ref, sh_ref, sc_ref, u_dst):
        u_dst[...] = (_layer_norm(x_ref[0]) * (1.0 + sc_ref[0]) + sh_ref[0]).astype(BF16)

    def project(u_src):
        p = _dot(u_src[...], w_ref[...])
        f = p[:, COL_F:COL_Q].astype(BF16)
        fcs_ref[0] = _dot(f, wch_ref[...]).astype(BF16)

        cos = cos_ref[...]
        sin_signed = sin_ref[...]
        lane = lax.broadcasted_iota(jnp.int32, cos.shape, 1)
        first_of_pair = jnp.bitwise_and(lane, 2 * (RET_QK_DIM // 4) - 1) < (RET_QK_DIM // 4)
        qk_ref[0, :, :RET_QK_WIDTH] = _rope(p[:, COL_Q:COL_K], cos, sin_signed, first_of_pair).astype(BF16)
        k = p[:, COL_K:COL_V] * (RET_QK_DIM ** -0.5)
        qk_ref[0, :, RET_QK_WIDTH:] = _rope(k, cos, sin_signed, first_of_pair).astype(BF16)
        vg_ref[0, :, :RET_WIDTH] = p[:, COL_V:COL_G].astype(BF16)
        vg_ref[0, :, RET_WIDTH:] = _silu(p[:, COL_G:PROJ_WIDTH]).astype(BF16)

    @pl.when(step == 0)
    def _():
        prepare(x0_ref, sh0_ref, sc0_ref, u_a)

    even = lax.rem(step, 2) == 0

    @pl.when(even)
    def _():
        prepare(xn_ref, shn_ref, scn_ref, u_b)
        project(u_a)

    @pl.when(jnp.logical_not(even))
    def _():
        prepare(xn_ref, shn_ref, scn_ref, u_a)
        project(u_b)


def _cast_block_rows(rows, n_steps):
    for r in range(BF16_SUBLANES, rows + 1, BF16_SUBLANES):
        if rows % r == 0 and rows // r <= n_steps:
            return r
    raise ValueError("no aligned row blocking for the weight cast")


def _in_proj(x, sh1, sc1, w_in_bf16, w_ch, cos_t, sin_t, ffn_weights):
    b, n, d = x.shape
    tm = min(IN_PROJ_TILE, n)
    tiles_per_batch = n // tm
    n_tiles = b * tiles_per_batch
    tok0, vec0, _ = _tile_specs(lambda s: 0 * s, tiles_per_batch, tm, d, pipeline_mode=pl.Buffered(1))
    tokn, vecn, _ = _tile_specs(lambda s: jnp.minimum(s + 1, n_tiles - 1), tiles_per_batch, tm, d)
    tok, _, rope = _tile_specs(lambda s: s, tiles_per_batch, tm, d)
    resident = lambda shape: pl.BlockSpec(shape, lambda s: (0, 0), pipeline_mode=pl.Buffered(1))

    cast_specs = []
    for w in ffn_weights:
        r = _cast_block_rows(w.shape[0], n_tiles)
        last = w.shape[0] // r - 1
        cast_specs.append(pl.BlockSpec((r, w.shape[1]), lambda s, last=last: (jnp.minimum(s, last), 0)))

    widths = (2 * RET_QK_WIDTH, 2 * RET_WIDTH, 2 * FOURIER_WIDTH)
    return pl.pallas_call(
        _in_proj_kernel,
        grid=(n_tiles,),
        in_specs=[
            tok0(d), vec0, vec0, tokn(d), vecn, vecn,
            resident((d, PROJ_WIDTH)), resident((FOURIER_WIDTH, 2 * FOURIER_WIDTH)),
            rope, rope, *cast_specs,
        ],
        out_specs=[tok(width) for width in widths] + cast_specs,
        out_shape=[jax.ShapeDtypeStruct((b, n, width), BF16) for width in widths]
        + [jax.ShapeDtypeStruct(w.shape, BF16) for w in ffn_weights],
        scratch_shapes=[pltpu.VMEM((tm, d), BF16), pltpu.VMEM((tm, d), BF16)],
        compiler_params=pltpu.CompilerParams(
            dimension_semantics=("arbitrary",), vmem_limit_bytes=V7X_VMEM_LIMIT_BYTES),
        name="in_proj",
    )(x, sh1, sc1, x, sh1, sc1, w_in_bf16, w_ch, cos_t, sin_t, *ffn_weights)


def _position_dft_sequence(fcs_ref, mat_ref, twc_ref, tws_ref, o_ref, bi):
    sub = mat_ref.shape[0]
    width = fcs_ref.shape[2] // 2
    reps = width // V7X_LANES
    mat = mat_ref[...].astype(BF16)
    c0, c1, c2, c3 = (fcs_ref[bi, j * sub:(j + 1) * sub, :width].astype(F32) for j in range(DFT_RADIX))
    s0, s1, s2, s3 = (fcs_ref[bi, j * sub:(j + 1) * sub, width:].astype(F32) for j in range(DFT_RADIX))
    tc0, tc1, tc2, tc3 = c0 + c2, c1 + c3, c0 - c2, c1 - c3
    ts0, ts1, ts2, ts3 = s0 + s2, s1 + s3, s0 - s2, s1 - s3
    parts = [
        (tc0 + tc1, ts0 + ts1),
        (tc2 - ts3, ts2 + tc3),
        (tc0 - tc1, ts0 - ts1),
        (tc2 + ts3, ts2 - tc3),
    ]
    for k1, (a, b) in enumerate(parts):
        if k1 > 0:
            cphi = jnp.concatenate([twc_ref[k1 - 1]] * reps, axis=1)
            sphi = jnp.concatenate([tws_ref[k1 - 1]] * reps, axis=1)
            a, b = a * cphi - b * sphi, a * sphi + b * cphi
        rhs = jnp.concatenate([a.astype(BF16), b.astype(BF16)], axis=0)
        y = _dot(mat, rhs)
        for h in range(reps):
            o_ref[bi, h, pl.ds(k1, sub, stride=DFT_RADIX), :] = y[:, h * V7X_LANES:(h + 1) * V7X_LANES]


def _mixer_kernel(qk_ref, vg_ref, stf0_ref, stb0_ref, dfk_ref, dbk_ref, dfv_ref, dbv_ref,
                  fcs_ref, mat_ref, twc_ref, tws_ref,
                  o_ref, fm_ref, u_all, s_all, tab_s, xi_s):
    _position_dft_sequence(fcs_ref, mat_ref, twc_ref, tws_ref, fm_ref, 0)

    n = qk_ref.shape[1]
    n_chunks = n // RET_CHUNK
    c = RET_CHUNK

    i_k = lax.broadcasted_iota(jnp.int32, (c, PAIR_QK), 0).astype(F32)
    i_v = lax.broadcasted_iota(jnp.int32, (c, PAIR_V), 0)
    j_v = jnp.bitwise_and(lax.broadcasted_iota(jnp.int32, (c, PAIR_V), 1), RET_V_DIM - 1)
    diff = (i_v - j_v).astype(F32)
    bd = _block_diag_mask((PAIR_QK, PAIR_V))
    lane_k = lax.broadcasted_iota(jnp.int32, (c, PAIR_QK), 1)
    lane_v = lax.broadcasted_iota(jnp.int32, (c, PAIR_V), 1)
    zero_k = jnp.zeros((c, PAIR_QK), BF16)
    zero_v = jnp.zeros((c, PAIR_V), BF16)

    for p in range(HEAD_PAIRS):
        rows = lambda i: slice(i * c, (i + 1) * c)
        q_chunk = lambda i: qk_ref[0, rows(i), p * PAIR_QK:(p + 1) * PAIR_QK]
        k_chunk = lambda i: qk_ref[0, rows(i), RET_QK_WIDTH + p * PAIR_QK:RET_QK_WIDTH + (p + 1) * PAIR_QK]
        v_chunk = lambda i: vg_ref[0, rows(i), p * PAIR_V:(p + 1) * PAIR_V]
        g_chunk = lambda i: vg_ref[0, rows(i), RET_WIDTH + p * PAIR_V:RET_WIDTH + (p + 1) * PAIR_V]

        lgf_k = _log_sigmoid(dfk_ref[p])
        lgb_k = _log_sigmoid(dbk_ref[p])
        lgf_v = _log_sigmoid(dfv_ref[p])
        lgb_v = _log_sigmoid(dbv_ref[p])

        tab_s[p] = jnp.where(diff >= 0.0, jnp.exp(lgf_v * jnp.maximum(diff, 0.0)),
                             jnp.exp(lgb_v * jnp.maximum(-diff, 0.0)))
        xi_s[p, 0] = jnp.exp(lgf_k * (i_k + 1.0))
        xi_s[p, 1] = jnp.exp(lgb_k * (c - i_k))
        g_f = jnp.exp(lgf_v * float(c))
        g_b = jnp.exp(lgb_v * float(c))
        zeta_f = jnp.exp(lgf_k * (c - 1.0 - i_k))
        zeta_b = jnp.exp(lgb_k * i_k)

        for i in range(n_chunks):
            k2 = k_chunk(i).astype(F32)
            kz = jnp.concatenate([(k2 * zeta_f).astype(BF16), (k2 * zeta_b).astype(BF16)], axis=1)
            u_all[p, i] = _dot_tt(kz, v_chunk(i), 0, 0)

        s_f = stf0_ref[0, p]
        s_b = stb0_ref[0, p]
        for i in range(n_chunks):
            j = n_chunks - 1 - i
            s_all[p, i, :PAIR_QK, :] = jnp.where(bd, s_f, 0.0).astype(BF16)
            s_all[p, j, PAIR_QK:, :] = jnp.where(bd, s_b, 0.0).astype(BF16)
            if i + 1 < n_chunks:
                s_f = g_f * s_f + u_all[p, i, :PAIR_QK, :]
                s_b = g_b * s_b + u_all[p, j, PAIR_QK:, :]

        for i in range(n_chunks):
            q2 = q_chunk(i)
            k2 = k_chunk(i)
            v2 = v_chunk(i)
            k_bd = jnp.concatenate([jnp.where(lane_k < RET_QK_DIM, k2, zero_k),
                                    jnp.where(lane_k >= RET_QK_DIM, k2, zero_k)], axis=0)
            v_bd = jnp.concatenate([jnp.where(lane_v < RET_V_DIM, v2, zero_v),
                                    jnp.where(lane_v >= RET_V_DIM, v2, zero_v)], axis=0)
            scores = _dot_tt(q2, k_bd, 1, 1)
            pr = (scores * tab_s[p]).astype(BF16)
            q2f = q2.astype(F32)
            qx = jnp.concatenate([(q2f * xi_s[p, 0]).astype(BF16), (q2f * xi_s[p, 1]).astype(BF16)], axis=1)
            y = _dot(pr, v_bd) + _dot(qx, s_all[p, i])
            g = g_chunk(i).astype(F32)
            outs = []
            for h in range(2):
                yh = y[:, h * RET_V_DIM:(h + 1) * RET_V_DIM]
                ms = jnp.mean(yh * yh, axis=-1, keepdims=True)
                outs.append(yh * lax.rsqrt(ms + LN_EPS) * g[:, h * RET_V_DIM:(h + 1) * RET_V_DIM])
            o_ref[0, rows(i), p * PAIR_V:(p + 1) * PAIR_V] = jnp.concatenate(outs, axis=1).astype(BF16)


def _mixer(qk, vg, fcs, st_f, st_b, dec_f_k, dec_b_k, dec_f_v, dec_b_v, mat, tw_cos, tw_sin):
    b, n, _ = qk.shape
    n_chunks = n // RET_CHUNK
    planes = fcs.shape[2] // 2 // V7X_LANES
    seq = lambda width: pl.BlockSpec((1, n, width), lambda bi: (bi, 0, 0))
    st = pl.BlockSpec((1, HEAD_PAIRS, PAIR_QK, PAIR_V), lambda bi: (bi, 0, 0, 0))
    dk = pl.BlockSpec((HEAD_PAIRS, 1, PAIR_QK), lambda bi: (0, 0, 0))
    dv = pl.BlockSpec((HEAD_PAIRS, 1, PAIR_V), lambda bi: (0, 0, 0))
    resident = lambda shape: pl.BlockSpec(shape, lambda bi: (0,) * len(shape), pipeline_mode=pl.Buffered(1))
    return pl.pallas_call(
        _mixer_kernel,
        grid=(b,),
        in_specs=[seq(2 * RET_QK_WIDTH), seq(2 * RET_WIDTH), st, st, dk, dk, dv, dv,
                  seq(fcs.shape[2]), resident(mat.shape), resident(tw_cos.shape), resident(tw_sin.shape)],
        out_specs=[seq(RET_WIDTH), pl.BlockSpec((1, planes, n, V7X_LANES), lambda bi: (bi, 0, 0, 0))],
        out_shape=[jax.ShapeDtypeStruct((b, n, RET_WIDTH), BF16),
                   jax.ShapeDtypeStruct((b, planes, n, V7X_LANES), F32)],
        scratch_shapes=[
            pltpu.VMEM((HEAD_PAIRS, n_chunks, 2 * PAIR_QK, PAIR_V), F32),
            pltpu.VMEM((HEAD_PAIRS, n_chunks, 2 * PAIR_QK, PAIR_V), BF16),
            pltpu.VMEM((HEAD_PAIRS, RET_CHUNK, PAIR_V), F32),
            pltpu.VMEM((HEAD_PAIRS, 2, RET_CHUNK, PAIR_QK), F32),
        ],
        compiler_params=pltpu.CompilerParams(vmem_limit_bytes=V7X_VMEM_LIMIT_BYTES),
        name="mixer",
    )(qk, vg, st_f, st_b, dec_f_k, dec_b_k, dec_f_v, dec_b_v, fcs, mat, tw_cos, tw_sin)


def _out_ffn_kernel(x0_ref, fm0_ref, yn0_ref, gt10_ref, sh20_ref, sc20_ref,
                    xb_ref, fmb_ref, ynb_ref, gt1b_ref, sh2b_ref, sc2b_ref,
                    xa_ref, fma_ref, yna_ref, gt1a_ref, sh2a_ref, sc2a_ref,
                    gt2a_ref, gt2b_ref, l1g_ref, l1b_ref, l2g_ref, l2b_ref, wo_ref, wg_ref, wu_ref, wd_ref,
                    o_ref, x1_a, x1_b, u2_a, u2_b):
    tm = x1_a.shape[0]

    def prepare(x_ref, fm_ref, yn_ref, gt1_ref, sh2_ref, sc2_ref, x1_dst, u2_dst):
        fm = jnp.concatenate([fm_ref[0, h] for h in range(fm_ref.shape[1])], axis=1).astype(BF16)
        mix = _dot(fm, wo_ref[:FOURIER_WIDTH, :]) + _dot(yn_ref[0], wo_ref[FOURIER_WIDTH:, :])
        x1 = _layer_norm(DEEPNORM_ALPHA * x_ref[0] + gt1_ref[0] * mix) * l1g_ref[...] + l1b_ref[...]
        x1_dst[...] = x1
        u2_dst[...] = (_layer_norm(x1) * (1.0 + sc2_ref[0]) + sh2_ref[0]).astype(BF16)

    def ffn(x1_src, u2_src, gt2_ref, row0):
        d_ff = wg_ref.shape[1]
        acc = None
        for c0 in range(0, d_ff, FFN_CHUNK):
            c1 = c0 + FFN_CHUNK
            u2 = u2_src[...]
            h = (_silu(_dot(u2, wg_ref[:, c0:c1])) * _dot(u2, wu_ref[:, c0:c1])).astype(BF16)
            part = _dot(h, wd_ref[c0:c1, :])
            acc = part if acc is None else acc + part
        z = DEEPNORM_ALPHA * x1_src[...] + gt2_ref[0] * acc
        o_ref[0, row0:row0 + tm, :] = _layer_norm(z) * l2g_ref[...] + l2b_ref[...]

    @pl.when(pl.program_id(0) == 0)
    def _():
        prepare(x0_ref, fm0_ref, yn0_ref, gt10_ref, sh20_ref, sc20_ref, x1_a, u2_a)

    prepare(xb_ref, fmb_ref, ynb_ref, gt1b_ref, sh2b_ref, sc2b_ref, x1_b, u2_b)
    ffn(x1_a, u2_a, gt2a_ref, 0)
    prepare(xa_ref, fma_ref, yna_ref, gt1a_ref, sh2a_ref, sc2a_ref, x1_a, u2_a)
    ffn(x1_b, u2_b, gt2b_ref, tm)


def _out_ffn(x, fm, yn, gt1, sh2, sc2, gt2, l1g, l1b, l2g, l2b, wo, wg, wu, wd):
    b, n, d = x.shape
    d_ff = wg.shape[1]
    assert d_ff % FFN_CHUNK == 0
    tm = min(TOKEN_TILE, n // 2)
    tiles_per_batch = n // tm
    n_tiles = b * tiles_per_batch
    assert tiles_per_batch % 2 == 0
    planes = fm.shape[1]

    def specs(tile_of_step, **kw):
        tok, vec, _ = _tile_specs(tile_of_step, tiles_per_batch, tm, d, **kw)
        bi = lambda s: tile_of_step(s) // tiles_per_batch
        ti = lambda s: tile_of_step(s) % tiles_per_batch
        four = pl.BlockSpec((1, planes, tm, V7X_LANES), lambda s: (bi(s), 0, ti(s), 0), **kw)
        return tok, vec, four

    tok0, vec0, four0 = specs(lambda g: 0 * g, pipeline_mode=pl.Buffered(1))
    tokb, vecb, fourb = specs(lambda g: 2 * g + 1)
    toka, veca, foura = specs(lambda g: jnp.minimum(2 * g + 2, n_tiles - 1))
    _, vecc, _ = specs(lambda g: 2 * g)
    par = pl.BlockSpec((1, d), lambda g: (0, 0))
    resident = lambda shape: pl.BlockSpec(shape, lambda g: (0, 0), pipeline_mode=pl.Buffered(1))
    pairs_per_batch = tiles_per_batch // 2
    out_spec = pl.BlockSpec((1, 2 * tm, d), lambda g: (g // pairs_per_batch, g % pairs_per_batch, 0))
    return pl.pallas_call(
        _out_ffn_kernel,
        grid=(n_tiles // 2,),
        in_specs=[
            tok0(d), four0, tok0(RET_WIDTH), vec0, vec0, vec0,
            tokb(d), fourb, tokb(RET_WIDTH), vecb, vecb, vecb,
            toka(d), foura, toka(RET_WIDTH), veca, veca, veca,
            vecc, vecb, par, par, par, par,
            resident((d, d)), resident((d, d_ff)), resident((d, d_ff)), resident((d_ff, d)),
        ],
        out_specs=out_spec,
        out_shape=jax.ShapeDtypeStruct((b, n, d), x.dtype),
        scratch_shapes=[
            pltpu.VMEM((tm, d), F32), pltpu.VMEM((tm, d), F32),
            pltpu.VMEM((tm, d), BF16), pltpu.VMEM((tm, d), BF16),
        ],
        compiler_params=pltpu.CompilerParams(
            dimension_semantics=("arbitrary",), vmem_limit_bytes=V7X_VMEM_LIMIT_BYTES),
        name="out_ffn",
    )(x, fm, yn, gt1, sh2, sc2, x, fm, yn, gt1, sh2, sc2, x, fm, yn, gt1, sh2, sc2, gt2, gt2,
      l1g, l1b, l2g, l2b, wo, wg, wu, wd)


def _pair_lanes(decay, width):
    return jnp.repeat(decay.astype(F32), width).reshape(HEAD_PAIRS, 1, 2 * width)


def kernel(x, c, ctx, c_ctx, w_mod, b_mod, w_in, w_out, decay_fwd, decay_bwd, ln1_g, ln1_b,
           w_ffn_gate, w_ffn_up, w_ffn_down, ln2_g, ln2_b):
    b, n, d = x.shape
    assert w_mod.shape[0] == DEPTH and n % RET_CHUNK == 0 and n % GRID_W == 0
    assert n % (V7X_SUBLANES * DFT_RADIX) == 0
    assert ctx.shape[1] % RET_CHUNK == 0 and w_in.shape[2] == PROJ_WIDTH

    rows = -(-(b + 1) // V7X_SUBLANES) * V7X_SUBLANES
    c_rows = jnp.zeros((rows, d), F32).at[:b].set(c).at[b].set(c_ctx)
    mod = _modulation(c_rows, w_mod[0], b_mod[0])
    sh1, sc1, gt1, sh2, sc2, gt2 = [mod[:b, i * d:(i + 1) * d].reshape(b, 1, d) for i in range(6)]
    csh1 = mod[b:b + 1, 0:d]
    csc1 = mod[b:b + 1, d:2 * d]

    dec_f_k = _pair_lanes(decay_fwd[0], RET_QK_DIM)
    dec_b_k = _pair_lanes(decay_bwd[0], RET_QK_DIM)
    dec_f_v = _pair_lanes(decay_fwd[0], RET_V_DIM)
    dec_b_v = _pair_lanes(decay_bwd[0], RET_V_DIM)

    st_f, st_b, w_in_bf16 = _ctx_states(ctx, csh1, csc1, w_in[0], dec_f_k, dec_b_k)

    cos_t, sin_t = _rope_tables(n)
    w_ch = jnp.asarray(_channel_dft_table(), dtype=F32).astype(BF16)
    qk, vg, fcs, wo, wg, wu, wd = _in_proj(
        x, sh1, sc1, w_in_bf16, w_ch, jnp.asarray(cos_t), jnp.asarray(sin_t),
        (w_out[0], w_ffn_gate[0], w_ffn_up[0], w_ffn_down[0]))

    dft_mat, tw_cos, tw_sin = (jnp.asarray(t, dtype=F32) for t in _position_dft_tables(n))
    yn, fm = _mixer(qk, vg, fcs, st_f, st_b, dec_f_k, dec_b_k, dec_f_v, dec_b_v, dft_mat, tw_cos, tw_sin)

    return _out_ffn(x, fm, yn, gt1, sh2, sc2, gt2,
                    ln1_g[0].reshape(1, d), ln1_b[0].reshape(1, d), ln2_g[0].reshape(1, d), ln2_b[0].reshape(1, d),
                    wo, wg, wu, wd)
```

```python
import functools

import numpy as np
import jax
import jax.numpy as jnp
from jax import lax
from jax.experimental import pallas as pl
from jax.experimental.pallas import tpu as pltpu

F32 = jnp.float32
BF16 = jnp.bfloat16

GRID_W = 64
FOURIER_GROUPS = 4
FOURIER_GROUP_DIM = 64
FOURIER_WIDTH = FOURIER_GROUPS * FOURIER_GROUP_DIM
RET_HEADS = 6
RET_QK_DIM = 64
RET_V_DIM = 128
RET_QK_WIDTH = RET_HEADS * RET_QK_DIM
RET_WIDTH = RET_HEADS * RET_V_DIM
RET_CHUNK = 128
HEAD_PAIRS = RET_HEADS // 2
PAIR_QK = 2 * RET_QK_DIM
PAIR_V = 2 * RET_V_DIM
ROPE_BASE = 10000.0
LN_EPS = 1e-6
DEPTH = 1
DEEPNORM_ALPHA = (2.0 * DEPTH) ** 0.25

COL_F = 0
COL_Q = COL_F + FOURIER_WIDTH
COL_K = COL_Q + RET_QK_WIDTH
COL_V = COL_K + RET_QK_WIDTH
COL_G = COL_V + RET_WIDTH
PROJ_WIDTH = COL_G + RET_WIDTH

V7X_LANES = 128
V7X_SUBLANES = 8
V7X_VMEM_LIMIT_BYTES = 56 * 1024 * 1024

TOKEN_TILE = 512
IN_PROJ_TILE = 1024
FFN_CHUNK = 256
DFT_RADIX = 4
CTX_BATCH_GROUP = 4
BF16_SUBLANES = 16


def _dot(a, b):
    return jnp.dot(a, b, preferred_element_type=F32)


def _dot_tt(a, b, lhs_dim, rhs_dim):
    return lax.dot_general(a, b, (((lhs_dim,), (rhs_dim,)), ((), ())), preferred_element_type=F32)


def _layer_norm(x):
    mu = jnp.mean(x, axis=-1, keepdims=True)
    xc = x - mu
    var = jnp.mean(xc * xc, axis=-1, keepdims=True)
    return xc * lax.rsqrt(var + LN_EPS)


def _silu(x):
    return x * jax.nn.sigmoid(x)


def _log_sigmoid(x):
    return jnp.minimum(x, 0.0) - jnp.log1p(jnp.exp(-jnp.abs(x)))


@functools.lru_cache(maxsize=None)
def _rope_tables(n_tokens):
    nf = RET_QK_DIM // 4
    t = np.arange(n_tokens)
    row = (t // GRID_W).astype(np.float32)
    col = (t % GRID_W).astype(np.float32)
    freqs = (np.float32(ROPE_BASE) ** (-np.arange(nf, dtype=np.float32) / np.float32(nf))).astype(np.float32)
    ang_r = (row[:, None] * freqs[None, :]).astype(np.float32)
    ang_c = (col[:, None] * freqs[None, :]).astype(np.float32)
    cos_h = np.concatenate([np.cos(ang_r), np.cos(ang_r), np.cos(ang_c), np.cos(ang_c)], axis=1)
    sin_h = np.concatenate([-np.sin(ang_r), np.sin(ang_r), -np.sin(ang_c), np.sin(ang_c)], axis=1)
    reps = V7X_LANES // RET_QK_DIM
    cos_t = np.tile(cos_h, (1, reps)).astype(np.float32)
    sin_t = np.tile(sin_h, (1, reps)).astype(np.float32)
    k_scale = np.float32(RET_QK_DIM ** -0.5)
    assert np.log2(k_scale) == np.round(np.log2(k_scale))
    return np.concatenate([cos_t, sin_t, cos_t * k_scale, sin_t * k_scale], axis=1)


@functools.lru_cache(maxsize=None)
def _channel_dft_table():
    m = np.arange(FOURIER_GROUP_DIM)
    ang = 2.0 * np.pi * ((m[:, None] * m[None, :]) % FOURIER_GROUP_DIM) / FOURIER_GROUP_DIM
    scale = FOURIER_GROUP_DIM ** -0.5
    eye = np.eye(FOURIER_GROUPS)
    return np.concatenate([np.kron(eye, np.cos(ang) * scale), np.kron(eye, np.sin(ang) * scale)], axis=1)


@functools.lru_cache(maxsize=None)
def _position_dft_tables(n_tokens):
    sub = n_tokens // DFT_RADIX
    m = np.arange(sub, dtype=np.int64)
    ang = 2.0 * np.pi * ((m[:, None] * m[None, :]) % sub) / sub
    scale = n_tokens ** -0.5
    mat = np.concatenate([np.cos(ang) * scale, -np.sin(ang) * scale], axis=1)
    k1 = np.arange(1, DFT_RADIX, dtype=np.int64)
    phi = 2.0 * np.pi * ((k1[:, None] * m[None, :]) % n_tokens) / n_tokens
    lanes = np.ones((1, 1, V7X_LANES))
    return mat, np.cos(phi)[:, :, None] * lanes, np.sin(phi)[:, :, None] * lanes


def _modulation_kernel(c_ref, w_ref, b_ref, o_ref):
    s = _silu(c_ref[...]).astype(BF16)
    o_ref[...] = _dot(s, w_ref[...].astype(BF16)) + b_ref[...]


def _modulation(c_rows, w_mod, b_mod):
    rows, d = c_rows.shape
    width = w_mod.shape[1]
    tn = width // 4
    return pl.pallas_call(
        _modulation_kernel,
        grid=(width // tn,),
        in_specs=[
            pl.BlockSpec((rows, d), lambda j: (0, 0)),
            pl.BlockSpec((d, tn), lambda j: (0, j)),
            pl.BlockSpec((1, tn), lambda j: (0, j)),
        ],
        out_specs=pl.BlockSpec((rows, tn), lambda j: (0, j)),
        out_shape=jax.ShapeDtypeStruct((rows, width), F32),
        compiler_params=pltpu.CompilerParams(vmem_limit_bytes=V7X_VMEM_LIMIT_BYTES),
        name="modulation",
    )(c_rows, w_mod, b_mod.reshape(1, width))


def _block_diag_mask(shape):
    r = lax.broadcasted_iota(jnp.int32, shape, 0)
    c = lax.broadcasted_iota(jnp.int32, shape, 1)
    return (r >= RET_QK_DIM) == (c >= RET_V_DIM)


def _ctx_states_kernel(ctx_ref, sh_ref, sc_ref, w_ref, dfk_ref, dbk_ref, stf_ref, stb_ref, wq_ref):
    group, length, d = ctx_ref.shape

    @pl.when(pl.program_id(0) == 0)
    def _():
        wq_ref[...] = w_ref[...].astype(BF16)

    x = ctx_ref[...].reshape(group * length, d)
    u = (_layer_norm(x) * (1.0 + sc_ref[...]) + sh_ref[...]).astype(BF16)
    kv = _dot(u, w_ref[:, COL_K:COL_G].astype(BF16))
    k = kv[:, :RET_QK_WIDTH] * (RET_QK_DIM ** -0.5)
    v = kv[:, RET_QK_WIDTH:].astype(BF16)
    t = lax.broadcasted_iota(jnp.int32, (length, PAIR_QK), 0).astype(F32)
    bd = _block_diag_mask((PAIR_QK, PAIR_V))
    for p in range(HEAD_PAIRS):
        w_f = jnp.exp(_log_sigmoid(dfk_ref[p]) * (length - 1.0 - t))
        w_b = jnp.exp(_log_sigmoid(dbk_ref[p]) * t)
        for bi in range(group):
            rows = slice(bi * length, (bi + 1) * length)
            k2 = k[rows, p * PAIR_QK:(p + 1) * PAIR_QK]
            v2 = v[rows, p * PAIR_V:(p + 1) * PAIR_V]
            st_f = _dot_tt((k2 * w_f).astype(BF16), v2, 0, 0)
            st_b = _dot_tt((k2 * w_b).astype(BF16), v2, 0, 0)
            stf_ref[bi, p] = jnp.where(bd, st_f, 0.0)
            stb_ref[bi, p] = jnp.where(bd, st_b, 0.0)


def _ctx_states(ctx, csh, csc, w_in, dec_f_k, dec_b_k):
    b, length, d = ctx.shape
    group = CTX_BATCH_GROUP if b % CTX_BATCH_GROUP == 0 else 1
    st_shape = jax.ShapeDtypeStruct((b, HEAD_PAIRS, PAIR_QK, PAIR_V), F32)
    st_spec = pl.BlockSpec((group, HEAD_PAIRS, PAIR_QK, PAIR_V), lambda i: (i, 0, 0, 0))
    dec_spec = pl.BlockSpec((HEAD_PAIRS, 1, PAIR_QK), lambda i: (0, 0, 0))
    return pl.pallas_call(
        _ctx_states_kernel,
        grid=(b // group,),
        in_specs=[
            pl.BlockSpec((group, length, d), lambda i: (i, 0, 0)),
            pl.BlockSpec((1, d), lambda i: (0, 0)),
            pl.BlockSpec((1, d), lambda i: (0, 0)),
            pl.BlockSpec((d, PROJ_WIDTH), lambda i: (0, 0), pipeline_mode=pl.Buffered(1)),
            dec_spec,
            dec_spec,
        ],
        out_specs=[st_spec, st_spec, pl.BlockSpec((d, PROJ_WIDTH), lambda i: (0, 0))],
        out_shape=[st_shape, st_shape, jax.ShapeDtypeStruct((d, PROJ_WIDTH), BF16)],
        compiler_params=pltpu.CompilerParams(
            dimension_semantics=("arbitrary",), vmem_limit_bytes=V7X_VMEM_LIMIT_BYTES),
        name="ctx_states",
    )(ctx, csh, csc, w_in, dec_f_k, dec_b_k)


def _rope(x, cos, sin_signed, first_of_pair):
    nf = RET_QK_DIM // 4
    outs = []
    for j in range(x.shape[1] // V7X_LANES):
        xj = x[:, j * V7X_LANES:(j + 1) * V7X_LANES]
        partner = jnp.where(first_of_pair, pltpu.roll(xj, V7X_LANES - nf, 1), pltpu.roll(xj, nf, 1))
        outs.append(xj * cos + partner * sin_signed)
    return jnp.concatenate(outs, axis=1)


def _tile_specs(tile_of_step, tiles_per_batch, tm, d, **kw):
    bi = lambda s: tile_of_step(s) // tiles_per_batch
    ti = lambda s: tile_of_step(s) % tiles_per_batch
    tok = lambda width: pl.BlockSpec((1, tm, width), lambda s: (bi(s), ti(s), 0), **kw)
    vec = pl.BlockSpec((1, 1, d), lambda s: (bi(s), 0, 0), **kw)
    rope = pl.BlockSpec((tm, 4 * V7X_LANES), lambda s: (ti(s), 0), **kw)
    return tok, vec, rope


def _in_proj_kernel(x0_ref, sh0_ref, sc0_ref, xn_ref, shn_ref, scn_ref, w_ref, wch_ref, rope_ref,
                    wo_ref, wg_ref, wu_ref, wd_ref,
                    qk_ref, vg_ref, fcs_ref, wo_out, wg_out, wu_out, wd_out, u_a, u_b):
    step = pl.program_id(0)
    for src, dst in ((wo_ref, wo_out), (wg_ref, wg_out), (wu_ref, wu_out), (wd_ref, wd_out)):
        dst[...] = src[...].astype(BF16)

    def prepare(x_ref, sh_ref, sc_ref, u_dst):
        u_dst[...] = (_layer_norm(x_ref[0]) * (1.0 + sc_ref[0]) + sh_ref[0]).astype(BF16)

    def project(u_src):
        p = _dot(u_src[...], w_ref[...])
        f = p[:, COL_F:COL_Q].astype(BF16)
        fcs_ref[0] = _dot(f, wch_ref[...]).astype(BF16)

        cos, sin_signed, cos_k, sin_k = (rope_ref[:, j * V7X_LANES:(j + 1) * V7X_LANES] for j in range(4))
        lane = lax.broadcasted_iota(jnp.int32, cos.shape, 1)
        first_of_pair = jnp.bitwise_and(lane, 2 * (RET_QK_DIM // 4) - 1) < (RET_QK_DIM // 4)
        qk_ref[0, :, :RET_QK_WIDTH] = _rope(p[:, COL_Q:COL_K], cos, sin_signed, first_of_pair).astype(BF16)
        qk_ref[0, :, RET_QK_WIDTH:] = _rope(p[:, COL_K:COL_V], cos_k, sin_k, first_of_pair).astype(BF16)
        vg_ref[0, :, :RET_WIDTH] = p[:, COL_V:COL_G].astype(BF16)
        vg_ref[0, :, RET_WIDTH:] = _silu(p[:, COL_G:PROJ_WIDTH]).astype(BF16)

    @pl.when(step == 0)
    def _():
        prepare(x0_ref, sh0_ref, sc0_ref, u_a)

    even = lax.rem(step, 2) == 0

    @pl.when(even)
    def _():
        prepare(xn_ref, shn_ref, scn_ref, u_b)
        project(u_a)

    @pl.when(jnp.logical_not(even))
    def _():
        prepare(xn_ref, shn_ref, scn_ref, u_a)
        project(u_b)


def _cast_block_rows(rows, n_steps):
    for r in range(BF16_SUBLANES, rows + 1, BF16_SUBLANES):
        if rows % r == 0 and rows // r <= n_steps:
            return r
    raise ValueError("no aligned row blocking for the weight cast")


def _in_proj(x, sh1, sc1, w_in_bf16, w_ch, rope_t, ffn_weights):
    b, n, d = x.shape
    tm = min(IN_PROJ_TILE, n)
    tiles_per_batch = n // tm
    n_tiles = b * tiles_per_batch
    tok0, vec0, _ = _tile_specs(lambda s: 0 * s, tiles_per_batch, tm, d, pipeline_mode=pl.Buffered(1))
    tokn, vecn, _ = _tile_specs(lambda s: jnp.minimum(s + 1, n_tiles - 1), tiles_per_batch, tm, d)
    tok, _, rope = _tile_specs(lambda s: s, tiles_per_batch, tm, d)
    resident = lambda shape: pl.BlockSpec(shape, lambda s: (0, 0), pipeline_mode=pl.Buffered(1))

    cast_specs = []
    for w in ffn_weights:
        r = _cast_block_rows(w.shape[0], n_tiles)
        last = w.shape[0] // r - 1
        cast_specs.append(pl.BlockSpec((r, w.shape[1]), lambda s, last=last: (jnp.minimum(s, last), 0)))

    widths = (2 * RET_QK_WIDTH, 2 * RET_WIDTH, 2 * FOURIER_WIDTH)
    return pl.pallas_call(
        _in_proj_kernel,
        grid=(n_tiles,),
        in_specs=[
            tok0(d), vec0, vec0, tokn(d), vecn, vecn,
            resident((d, PROJ_WIDTH)), resident((FOURIER_WIDTH, 2 * FOURIER_WIDTH)),
            rope, *cast_specs,
        ],
        out_specs=[tok(width) for width in widths] + cast_specs,
        out_shape=[jax.ShapeDtypeStruct((b, n, width), BF16) for width in widths]
        + [jax.ShapeDtypeStruct(w.shape, BF16) for w in ffn_weights],
        scratch_shapes=[pltpu.VMEM((tm, d), BF16), pltpu.VMEM((tm, d), BF16)],
        compiler_params=pltpu.CompilerParams(
            dimension_semantics=("arbitrary",), vmem_limit_bytes=V7X_VMEM_LIMIT_BYTES),
        name="in_proj",
    )(x, sh1, sc1, x, sh1, sc1, w_in_bf16, w_ch, rope_t, *ffn_weights)


def _position_dft_sequence(fcs_ref, mat_ref, twc_ref, tws_ref, o_ref, bi):
    sub = mat_ref.shape[0]
    width = fcs_ref.shape[2] // 2
    reps = width // V7X_LANES
    mat = mat_ref[...].astype(BF16)
    c0, c1, c2, c3 = (fcs_ref[bi, j * sub:(j + 1) * sub, :width].astype(F32) for j in range(DFT_RADIX))
    s0, s1, s2, s3 = (fcs_ref[bi, j * sub:(j + 1) * sub, width:].astype(F32) for j in range(DFT_RADIX))
    tc0, tc1, tc2, tc3 = c0 + c2, c1 + c3, c0 - c2, c1 - c3
    ts0, ts1, ts2, ts3 = s0 + s2, s1 + s3, s0 - s2, s1 - s3
    parts = [
        (tc0 + tc1, ts0 + ts1),
        (tc2 - ts3, ts2 + tc3),
        (tc0 - tc1, ts0 - ts1),
        (tc2 + ts3, ts2 - tc3),
    ]
    for k1, (a, b) in enumerate(parts):
        if k1 > 0:
            cphi = jnp.concatenate([twc_ref[k1 - 1]] * reps, axis=1)
            sphi = jnp.concatenate([tws_ref[k1 - 1]] * reps, axis=1)
            a, b = a * cphi - b * sphi, a * sphi + b * cphi
        rhs = jnp.concatenate([a.astype(BF16), b.astype(BF16)], axis=0)
        y = _dot(mat, rhs)
        for h in range(reps):
            o_ref[bi, h, pl.ds(k1, sub, stride=DFT_RADIX), :] = y[:, h * V7X_LANES:(h + 1) * V7X_LANES]


def _mixer_kernel(qk_ref, vg_ref, stf0_ref, stb0_ref, dfk_ref, dbk_ref, dfv_ref, dbv_ref,
                  fcs_ref, mat_ref, twc_ref, tws_ref,
                  o_ref, fm_ref, u_all, s_all, tab_s, xi_s):
    _position_dft_sequence(fcs_ref, mat_ref, twc_ref, tws_ref, fm_ref, 0)

    n = qk_ref.shape[1]
    n_chunks = n // RET_CHUNK
    c = RET_CHUNK

    i_k = lax.broadcasted_iota(jnp.int32, (c, PAIR_QK), 0).astype(F32)
    i_v = lax.broadcasted_iota(jnp.int32, (c, PAIR_V), 0)
    j_v = jnp.bitwise_and(lax.broadcasted_iota(jnp.int32, (c, PAIR_V), 1), RET_V_DIM - 1)
    diff = (i_v - j_v).astype(F32)
    bd = _block_diag_mask((PAIR_QK, PAIR_V))
    lane_k = lax.broadcasted_iota(jnp.int32, (c, PAIR_QK), 1)
    lane_v = lax.broadcasted_iota(jnp.int32, (c, PAIR_V), 1)
    zero_k = jnp.zeros((c, PAIR_QK), BF16)
    zero_v = jnp.zeros((c, PAIR_V), BF16)

    for p in range(HEAD_PAIRS):
        rows = lambda i: slice(i * c, (i + 1) * c)
        q_chunk = lambda i: qk_ref[0, rows(i), p * PAIR_QK:(p + 1) * PAIR_QK]
        k_chunk = lambda i: qk_ref[0, rows(i), RET_QK_WIDTH + p * PAIR_QK:RET_QK_WIDTH + (p + 1) * PAIR_QK]
        v_chunk = lambda i: vg_ref[0, rows(i), p * PAIR_V:(p + 1) * PAIR_V]
        g_chunk = lambda i: vg_ref[0, rows(i), RET_WIDTH + p * PAIR_V:RET_WIDTH + (p + 1) * PAIR_V]

        lgf_k = _log_sigmoid(dfk_ref[p])
        lgb_k = _log_sigmoid(dbk_ref[p])
        lgf_v = _log_sigmoid(dfv_ref[p])
        lgb_v = _log_sigmoid(dbv_ref[p])

        tab_s[p] = jnp.where(diff >= 0.0, jnp.exp(lgf_v * jnp.maximum(diff, 0.0)),
                             jnp.exp(lgb_v * jnp.maximum(-diff, 0.0)))
        xi_s[p, 0] = jnp.exp(lgf_k * (i_k + 1.0))
        xi_s[p, 1] = jnp.exp(lgb_k * (c - i_k))
        g_f = jnp.exp(lgf_v * float(c))
        g_b = jnp.exp(lgb_v * float(c))
        zeta_f = jnp.exp(lgf_k * (c - 1.0 - i_k))
        zeta_b = jnp.exp(lgb_k * i_k)

        for i in range(n_chunks):
            k2 = k_chunk(i).astype(F32)
            kz = jnp.concatenate([(k2 * zeta_f).astype(BF16), (k2 * zeta_b).astype(BF16)], axis=1)
            u_all[p, i] = _dot_tt(kz, v_chunk(i), 0, 0)

        s_f = stf0_ref[0, p]
        s_b = stb0_ref[0, p]
        for i in range(n_chunks):
            j = n_chunks - 1 - i
            s_all[p, i, :PAIR_QK, :] = jnp.where(bd, s_f, 0.0).astype(BF16)
            s_all[p, j, PAIR_QK:, :] = jnp.where(bd, s_b, 0.0).astype(BF16)
            if i + 1 < n_chunks:
                s_f = g_f * s_f + u_all[p, i, :PAIR_QK, :]
                s_b = g_b * s_b + u_all[p, j, PAIR_QK:, :]

        for i in range(n_chunks):
            q2 = q_chunk(i)
            k2 = k_chunk(i)
            v2 = v_chunk(i)
            k_bd = jnp.concatenate([jnp.where(lane_k < RET_QK_DIM, k2, zero_k),
                                    jnp.where(lane_k >= RET_QK_DIM, k2, zero_k)], axis=0)
            v_bd = jnp.concatenate([jnp.where(lane_v < RET_V_DIM, v2, zero_v),
                                    jnp.where(lane_v >= RET_V_DIM, v2, zero_v)], axis=0)
            scores = _dot_tt(q2, k_bd, 1, 1)
            pr = (scores * tab_s[p]).astype(BF16)
            q2f = q2.astype(F32)
            qx = jnp.concatenate([(q2f * xi_s[p, 0]).astype(BF16), (q2f * xi_s[p, 1]).astype(BF16)], axis=1)
            y = _dot(pr, v_bd) + _dot(qx, s_all[p, i])
            g = g_chunk(i).astype(F32)
            outs = []
            for h in range(2):
                yh = y[:, h * RET_V_DIM:(h + 1) * RET_V_DIM]
                ms = jnp.mean(yh * yh, axis=-1, keepdims=True)
                outs.append(yh * lax.rsqrt(ms + LN_EPS) * g[:, h * RET_V_DIM:(h + 1) * RET_V_DIM])
            o_ref[0, rows(i), p * PAIR_V:(p + 1) * PAIR_V] = jnp.concatenate(outs, axis=1).astype(BF16)


def _mixer(qk, vg, fcs, st_f, st_b, dec_f_k, dec_b_k, dec_f_v, dec_b_v, mat, tw_cos, tw_sin):
    b, n, _ = qk.shape
    n_chunks = n // RET_CHUNK
    planes = fcs.shape[2] // 2 // V7X_LANES
    seq = lambda width: pl.BlockSpec((1, n, width), lambda bi: (bi, 0, 0))
    st = pl.BlockSpec((1, HEAD_PAIRS, PAIR_QK, PAIR_V), lambda bi: (bi, 0, 0, 0))
    dk = pl.BlockSpec((HEAD_PAIRS, 1, PAIR_QK), lambda bi: (0, 0, 0))
    dv = pl.BlockSpec((HEAD_PAIRS, 1, PAIR_V), lambda bi: (0, 0, 0))
    resident = lambda shape: pl.BlockSpec(shape, lambda bi: (0,) * len(shape), pipeline_mode=pl.Buffered(1))
    return pl.pallas_call(
        _mixer_kernel,
        grid=(b,),
        in_specs=[seq(2 * RET_QK_WIDTH), seq(2 * RET_WIDTH), st, st, dk, dk, dv, dv,
                  seq(fcs.shape[2]), resident(mat.shape), resident(tw_cos.shape), resident(tw_sin.shape)],
        out_specs=[seq(RET_WIDTH), pl.BlockSpec((1, planes, n, V7X_LANES), lambda bi: (bi, 0, 0, 0))],
        out_shape=[jax.ShapeDtypeStruct((b, n, RET_WIDTH), BF16),
                   jax.ShapeDtypeStruct((b, planes, n, V7X_LANES), F32)],
        scratch_shapes=[
            pltpu.VMEM((HEAD_PAIRS, n_chunks, 2 * PAIR_QK, PAIR_V), F32),
            pltpu.VMEM((HEAD_PAIRS, n_chunks, 2 * PAIR_QK, PAIR_V), BF16),
            pltpu.VMEM((HEAD_PAIRS, RET_CHUNK, PAIR_V), F32),
            pltpu.VMEM((HEAD_PAIRS, 2, RET_CHUNK, PAIR_QK), F32),
        ],
        compiler_params=pltpu.CompilerParams(vmem_limit_bytes=V7X_VMEM_LIMIT_BYTES),
        name="mixer",
    )(qk, vg, st_f, st_b, dec_f_k, dec_b_k, dec_f_v, dec_b_v, fcs, mat, tw_cos, tw_sin)


def _out_ffn_kernel(x0_ref, fm0_ref, yn0_ref, gt10_ref, sh20_ref, sc20_ref,
                    xb_ref, fmb_ref, ynb_ref, gt1b_ref, sh2b_ref, sc2b_ref,
                    xa_ref, fma_ref, yna_ref, gt1a_ref, sh2a_ref, sc2a_ref,
                    gt2a_ref, gt2b_ref, l1g_ref, l1b_ref, l2g_ref, l2b_ref, wo_ref, wg_ref, wu_ref, wd_ref,
                    o_ref, x1_a, x1_b, u2_a, u2_b):
    tm = x1_a.shape[0]

    def prepare(x_ref, fm_ref, yn_ref, gt1_ref, sh2_ref, sc2_ref, x1_dst, u2_dst):
        fm = jnp.concatenate([fm_ref[0, h] for h in range(fm_ref.shape[1])], axis=1).astype(BF16)
        mix = _dot(fm, wo_ref[:FOURIER_WIDTH, :]) + _dot(yn_ref[0], wo_ref[FOURIER_WIDTH:, :])
        x1 = _layer_norm(DEEPNORM_ALPHA * x_ref[0] + gt1_ref[0] * mix) * l1g_ref[...] + l1b_ref[...]
        x1_dst[...] = x1
        u2_dst[...] = (_layer_norm(x1) * (1.0 + sc2_ref[0]) + sh2_ref[0]).astype(BF16)

    def ffn(x1_src, u2_src, gt2_ref, row0):
        d_ff = wg_ref.shape[1]
        acc = None
        for c0 in range(0, d_ff, FFN_CHUNK):
            c1 = c0 + FFN_CHUNK
            u2 = u2_src[...]
            h = (_silu(_dot(u2, wg_ref[:, c0:c1])) * _dot(u2, wu_ref[:, c0:c1])).astype(BF16)
            part = _dot(h, wd_ref[c0:c1, :])
            acc = part if acc is None else acc + part
        z = DEEPNORM_ALPHA * x1_src[...] + gt2_ref[0] * acc
        o_ref[0, row0:row0 + tm, :] = _layer_norm(z) * l2g_ref[...] + l2b_ref[...]

    @pl.when(pl.program_id(0) == 0)
    def _():
        prepare(x0_ref, fm0_ref, yn0_ref, gt10_ref, sh20_ref, sc20_ref, x1_a, u2_a)

    prepare(xb_ref, fmb_ref, ynb_ref, gt1b_ref, sh2b_ref, sc2b_ref, x1_b, u2_b)
    ffn(x1_a, u2_a, gt2a_ref, 0)
    prepare(xa_ref, fma_ref, yna_ref, gt1a_ref, sh2a_ref, sc2a_ref, x1_a, u2_a)
    ffn(x1_b, u2_b, gt2b_ref, tm)


def _out_ffn(x, fm, yn, gt1, sh2, sc2, gt2, l1g, l1b, l2g, l2b, wo, wg, wu, wd):
    b, n, d = x.shape
    d_ff = wg.shape[1]
    assert d_ff % FFN_CHUNK == 0
    tm = min(TOKEN_TILE, n // 2)
    tiles_per_batch = n // tm
    n_tiles = b * tiles_per_batch
    assert tiles_per_batch % 2 == 0
    planes = fm.shape[1]

    def specs(tile_of_step, **kw):
        tok, vec, _ = _tile_specs(tile_of_step, tiles_per_batch, tm, d, **kw)
        bi = lambda s: tile_of_step(s) // tiles_per_batch
        ti = lambda s: tile_of_step(s) % tiles_per_batch
        four = pl.BlockSpec((1, planes, tm, V7X_LANES), lambda s: (bi(s), 0, ti(s), 0), **kw)
        return tok, vec, four

    tok0, vec0, four0 = specs(lambda g: 0 * g, pipeline_mode=pl.Buffered(1))
    tokb, vecb, fourb = specs(lambda g: 2 * g + 1)
    toka, veca, foura = specs(lambda g: jnp.minimum(2 * g + 2, n_tiles - 1))
    _, vecc, _ = specs(lambda g: 2 * g)
    par = pl.BlockSpec((1, d), lambda g: (0, 0))
    resident = lambda shape: pl.BlockSpec(shape, lambda g: (0, 0), pipeline_mode=pl.Buffered(1))
    pairs_per_batch = tiles_per_batch // 2
    out_spec = pl.BlockSpec((1, 2 * tm, d), lambda g: (g // pairs_per_batch, g % pairs_per_batch, 0))
    return pl.pallas_call(
        _out_ffn_kernel,
        grid=(n_tiles // 2,),
        in_specs=[
            tok0(d), four0, tok0(RET_WIDTH), vec0, vec0, vec0,
            tokb(d), fourb, tokb(RET_WIDTH), vecb, vecb, vecb,
            toka(d), foura, toka(RET_WIDTH), veca, veca, veca,
            vecc, vecb, par, par, par, par,
            resident((d, d)), resident((d, d_ff)), resident((d, d_ff)), resident((d_ff, d)),
        ],
        out_specs=out_spec,
        out_shape=jax.ShapeDtypeStruct((b, n, d), x.dtype),
        scratch_shapes=[
            pltpu.VMEM((tm, d), F32), pltpu.VMEM((tm, d), F32),
            pltpu.VMEM((tm, d), BF16), pltpu.VMEM((tm, d), BF16),
        ],
        compiler_params=pltpu.CompilerParams(
            dimension_semantics=("arbitrary",), vmem_limit_bytes=V7X_VMEM_LIMIT_BYTES),
        name="out_ffn",
    )(x, fm, yn, gt1, sh2, sc2, x, fm, yn, gt1, sh2, sc2, x, fm, yn, gt1, sh2, sc2, gt2, gt2,
      l1g, l1b, l2g, l2b, wo, wg, wu, wd)


def _pair_lanes(decay, width):
    return jnp.repeat(decay.astype(F32), width).reshape(HEAD_PAIRS, 1, 2 * width)


def kernel(x, c, ctx, c_ctx, w_mod, b_mod, w_in, w_out, decay_fwd, decay_bwd, ln1_g, ln1_b,
           w_ffn_gate, w_ffn_up, w_ffn_down, ln2_g, ln2_b):
    b, n, d = x.shape
    assert w_mod.shape[0] == DEPTH and n % RET_CHUNK == 0 and n % GRID_W == 0
    assert n % (V7X_SUBLANES * DFT_RADIX) == 0
    assert ctx.shape[1] % RET_CHUNK == 0 and w_in.shape[2] == PROJ_WIDTH

    rows = -(-(b + 1) // V7X_SUBLANES) * V7X_SUBLANES
    c_rows = jnp.zeros((rows, d), F32).at[:b].set(c).at[b].set(c_ctx)
    mod = _modulation(c_rows, w_mod[0], b_mod[0])
    sh1, sc1, gt1, sh2, sc2, gt2 = [mod[:b, i * d:(i + 1) * d].reshape(b, 1, d) for i in range(6)]
    csh1 = mod[b:b + 1, 0:d]
    csc1 = mod[b:b + 1, d:2 * d]

    dec_f_k = _pair_lanes(decay_fwd[0], RET_QK_DIM)
    dec_b_k = _pair_lanes(decay_bwd[0], RET_QK_DIM)
    dec_f_v = _pair_lanes(decay_fwd[0], RET_V_DIM)
    dec_b_v = _pair_lanes(decay_bwd[0], RET_V_DIM)

    st_f, st_b, w_in_bf16 = _ctx_states(ctx, csh1, csc1, w_in[0], dec_f_k, dec_b_k)

    w_ch = jnp.asarray(_channel_dft_table(), dtype=F32).astype(BF16)
    qk, vg, fcs, wo, wg, wu, wd = _in_proj(
        x, sh1, sc1, w_in_bf16, w_ch, jnp.asarray(_rope_tables(n)),
        (w_out[0], w_ffn_gate[0], w_ffn_up[0], w_ffn_down[0]))

    dft_mat, tw_cos, tw_sin = (jnp.asarray(t, dtype=F32) for t in _position_dft_tables(n))
    yn, fm = _mixer(qk, vg, fcs, st_f, st_b, dec_f_k, dec_b_k, dec_f_v, dec_b_v, dft_mat, tw_cos, tw_sin)

    return _out_ffn(x, fm, yn, gt1, sh2, sc2, gt2,
                    ln1_g[0].reshape(1, d), ln1_b[0].reshape(1, d), ln2_g[0].reshape(1, d), ln2_b[0].reshape(1, d),
                    wo, wg, wu, wd)
```
